```python
import jax, jax.numpy as jnp
from jax import lax
import numpy as np

D_MODEL = 2048
BATCH = 2
SEQ = 8192
DEPTH = 1

GRID_W = 64
CTX_LEN = 256
D_FF = 5632
N_MOD = 9
RET_HEADS = 8
RET_DK = 64
RET_DV = 128
RET_CHUNK = 128
RET_ROPE_BASE = 10000.0
MLA_HEADS = 8
MLA_Q_RANK = 512
MLA_KV_RANK = 256
MLA_NOPE = 128
MLA_ROPE = 64
MLA_V = 128
AXIAL_BASE = 10000.0
Q_BLOCK = 128
RMS_EPS = 1e-6
GN_EPS = 1e-5
MIX_OUT = RET_HEADS * RET_DV + MLA_HEADS * MLA_V
SPLITS = (RET_HEADS * RET_DK, RET_HEADS * RET_DK, RET_HEADS * RET_DV, RET_HEADS * RET_DV,
          MLA_Q_RANK, MLA_KV_RANK, MLA_ROPE)
MIX_IN = sum(SPLITS)

kernel_name = "hymba_retention_mla_macaron_dit"


def rms_norm(x, g):
    xf = x.astype(jnp.float32)
    y = xf * lax.rsqrt(jnp.mean(xf * xf, axis=-1, keepdims=True) + RMS_EPS)
    return (y * g.astype(jnp.float32)).astype(x.dtype)


def modulate(h, shift, scale):
    return h * (1.0 + scale) + shift


def swiglu(h, w_in, w_out):
    g, u = jnp.split(h @ w_in, 2, axis=-1)
    return (jax.nn.silu(g) * u) @ w_out


def rope_tables(pos, dim, base):
    inv = base ** (-jnp.arange(0, dim, 2, dtype=jnp.float32) / dim)
    ang = pos.astype(jnp.float32)[:, None] * inv[None, :]
    return jnp.cos(ang)[:, None, :], jnp.sin(ang)[:, None, :]


def rotate(x, cos, sin):
    x1, x2 = jnp.split(x, 2, axis=-1)
    return jnp.concatenate([x1 * cos - x2 * sin, x2 * cos + x1 * sin], axis=-1).astype(x.dtype)


def axial_rope(x, row_tab, col_tab):
    xr, xc = jnp.split(x, 2, axis=-1)
    return jnp.concatenate([rotate(xr, *row_tab), rotate(xc, *col_tab)], axis=-1)


def project(h, w_in, q_norm_g, w_uq, kv_norm_g, w_ukv):
    b, n, _ = h.shape
    idx = [int(i) for i in np.cumsum(SPLITS)[:-1]]
    rq, rk, rv, rg, cq, ckv, kr = jnp.split(h @ w_in, idx, axis=-1)
    rq = rq.reshape(b, n, RET_HEADS, RET_DK)
    rk = rk.reshape(b, n, RET_HEADS, RET_DK) * (RET_DK ** -0.5)
    rv = rv.reshape(b, n, RET_HEADS, RET_DV)
    q = (rms_norm(cq, q_norm_g) @ w_uq).reshape(b, n, MLA_HEADS, MLA_NOPE + MLA_ROPE)
    kv = (rms_norm(ckv, kv_norm_g) @ w_ukv).reshape(b, n, MLA_HEADS, MLA_NOPE + MLA_V)
    q_nope, q_rope = q[..., :MLA_NOPE], q[..., MLA_NOPE:]
    k_nope, v = kv[..., :MLA_NOPE], kv[..., MLA_NOPE:]
    k_rope = kr[:, :, None, :]
    return rq, rk, rv, rg, q_nope, q_rope, k_nope, k_rope, v


def mla_keys(k_nope, k_rope):
    k_rope = jnp.broadcast_to(k_rope, k_nope.shape[:-1] + (MLA_ROPE,))
    return jnp.concatenate([k_nope, k_rope], axis=-1)


def block_attention(q, k, v):
    b, nq, h, dq = q.shape
    nb = nq // Q_BLOCK
    scale = dq ** -0.5
    qb = jnp.moveaxis(q.reshape(b, nb, Q_BLOCK, h, dq), 1, 0)

    def one(qi):
        s = jnp.einsum('bqhd,bkhd->bhqk', qi, k).astype(jnp.float32) * scale
        p = jax.nn.softmax(s, axis=-1)
        return jnp.einsum('bhqk,bkhe->bqhe', p.astype(v.dtype), v)

    out = lax.map(one, qb)
    return jnp.moveaxis(out, 0, 1).reshape(b, nq, h * v.shape[-1])


def retention_chunked(q, k, v, log_gamma, s0, strict):
    b, h, n, dk = q.shape
    dv = v.shape[-1]
    cs = RET_CHUNK
    nc = n // cs
    qc = q.reshape(b, h, nc, cs, dk)
    kc = k.reshape(b, h, nc, cs, dk)
    vc = v.reshape(b, h, nc, cs, dv)
    idx = jnp.arange(cs, dtype=jnp.float32)
    diff = idx[:, None] - idx[None, :]
    mask = diff > 0 if strict else diff >= 0
    lg = log_gamma[:, None, None]
    dmat = jnp.where(mask[None], jnp.exp(lg * jnp.where(mask, diff, 0.0)[None]), 0.0)
    scores = jnp.einsum('bhnid,bhnjd->bhnij', qc, kc) * dmat[None, :, None]
    y_inner = jnp.einsum('bhnij,bhnje->bhnie', scores, vc)
    zeta = jnp.exp(log_gamma[:, None] * (cs - 1 - idx)[None, :])
    xi = jnp.exp(log_gamma[:, None] * (idx + 1.0)[None, :])
    kv = jnp.einsum('bhnjd,bhnje->bhnde', kc * zeta[None, :, None, :, None], vc)
    decay_chunk = jnp.exp(log_gamma * cs)[None, :, None, None]

    def step(s, kv_n):
        return decay_chunk * s + kv_n, s

    s_final, s_prev = lax.scan(step, s0, jnp.moveaxis(kv, 2, 0))
    s_prev = jnp.moveaxis(s_prev, 0, 2)
    y_cross = jnp.einsum('bhnid,bhnde->bhnie', qc * xi[None, :, None, :, None], s_prev)
    return (y_inner + y_cross).reshape(b, h, n, dv), s_final


def bidir_retention(q, k, v, lg_f, lg_b, s0_f, s0_b):
    y_f, s_f = retention_chunked(q, k, v, lg_f, s0_f, False)
    y_b, s_b = retention_chunked(jnp.flip(q, 2), jnp.flip(k, 2), jnp.flip(v, 2), lg_b, s0_b, True)
    return y_f + jnp.flip(y_b, 2), s_f, s_b


def retention_output(y, gate):
    mu = jnp.mean(y, axis=-1, keepdims=True)
    var = jnp.mean(jnp.square(y - mu), axis=-1, keepdims=True)
    y = (y - mu) * lax.rsqrt(var + GN_EPS)
    b, h, n, dv = y.shape
    y = jnp.swapaxes(y, 1, 2).reshape(b, n, h * dv)
    return (jax.nn.silu(gate.astype(jnp.float32)) * y).astype(gate.dtype)


def setup_inputs(seed: int = 0) -> dict:
    key = jax.random.key(seed)
    ks = jax.random.split(key, 24)
    f32 = jnp.float32

    def nrm(k, shape, scale):
        return jax.random.normal(k, shape, f32) * scale

    gam = 1.0 - 2.0 ** (-5.0 - jnp.arange(RET_HEADS, dtype=f32))
    decay_logit = jnp.log(gam) - jnp.log1p(-gam)
    return {
        "x": nrm(ks[0], (BATCH, SEQ, D_MODEL), 1.0),
        "c": nrm(ks[1], (BATCH, D_MODEL), 1.0),
        "ctx": nrm(ks[2], (BATCH, CTX_LEN, D_MODEL), 1.0),
        "c_ctx": nrm(ks[3], (D_MODEL,), 1.0),
        "ada_w": nrm(ks[4], (DEPTH, D_MODEL, N_MOD * D_MODEL), 0.5 * D_MODEL ** -0.5),
        "ada_b": nrm(ks[5], (DEPTH, N_MOD * D_MODEL), 0.02),
        "norm1_g": 1.0 + nrm(ks[6], (DEPTH, D_MODEL), 0.02),
        "ffn1_w_in": nrm(ks[7], (DEPTH, D_MODEL, 2 * D_FF), D_MODEL ** -0.5),
        "ffn1_w_out": nrm(ks[8], (DEPTH, D_FF, D_MODEL), D_FF ** -0.5),
        "norm2_g": 1.0 + nrm(ks[9], (DEPTH, D_MODEL), 0.02),
        "mix_w_in": nrm(ks[10], (DEPTH, D_MODEL, MIX_IN), D_MODEL ** -0.5),
        "ret_decay_fwd": decay_logit[None, :] + nrm(ks[11], (DEPTH, RET_HEADS), 0.05),
        "ret_decay_bwd": decay_logit[None, :] + nrm(ks[12], (DEPTH, RET_HEADS), 0.05),
        "mla_q_norm_g": 1.0 + nrm(ks[13], (DEPTH, MLA_Q_RANK), 0.02),
        "mla_w_uq": nrm(ks[14], (DEPTH, MLA_Q_RANK, MLA_HEADS * (MLA_NOPE + MLA_ROPE)), MLA_Q_RANK ** -0.5),
        "mla_kv_norm_g": 1.0 + nrm(ks[15], (DEPTH, MLA_KV_RANK), 0.02),
        "mla_w_ukv": nrm(ks[16], (DEPTH, MLA_KV_RANK, MLA_HEADS * (MLA_NOPE + MLA_V)), MLA_KV_RANK ** -0.5),
        "mix_w_out": nrm(ks[17], (DEPTH, MIX_OUT, D_MODEL), MIX_OUT ** -0.5),
        "norm3_g": 1.0 + nrm(ks[18], (DEPTH, D_MODEL), 0.02),
        "ffn2_w_in": nrm(ks[19], (DEPTH, D_MODEL, 2 * D_FF), D_MODEL ** -0.5),
        "ffn2_w_out": nrm(ks[20], (DEPTH, D_FF, D_MODEL), D_FF ** -0.5),
        "final_norm_g": 1.0 + nrm(ks[21], (D_MODEL,), 0.02),
    }


def reference(x, c, ctx, c_ctx, ada_w, ada_b, norm1_g, ffn1_w_in, ffn1_w_out, norm2_g,
              mix_w_in, ret_decay_fwd, ret_decay_bwd, mla_q_norm_g, mla_w_uq, mla_kv_norm_g,
              mla_w_ukv, mix_w_out, norm3_g, ffn2_w_in, ffn2_w_out, final_norm_g):
    b, n_lat, d = x.shape
    n_rows = n_lat // GRID_W
    pos_row = jnp.repeat(jnp.arange(n_rows), GRID_W)
    pos_col = jnp.tile(jnp.arange(GRID_W), n_rows)
    row_tab = rope_tables(pos_row, MLA_ROPE // 2, AXIAL_BASE)
    col_tab = rope_tables(pos_col, MLA_ROPE // 2, AXIAL_BASE)
    ret_tab = rope_tables(jnp.arange(n_lat), RET_DK, RET_ROPE_BASE)
    s_zero = jnp.zeros((b, RET_HEADS, RET_DK, RET_DV), jnp.float32)

    def heads(t):
        return jnp.swapaxes(t, 1, 2).astype(jnp.float32)

    for l in range(DEPTH):
        is_last = l == DEPTH - 1
        m_lat = (jax.nn.silu(c) @ ada_w[l] + ada_b[l]).reshape(b, N_MOD, 1, d)
        m_ctx = (jax.nn.silu(c_ctx)[None, :] @ ada_w[l] + ada_b[l]).reshape(1, N_MOD, 1, d)

        x = x + 0.5 * m_lat[:, 2] * swiglu(modulate(rms_norm(x, norm1_g[l]), m_lat[:, 0], m_lat[:, 1]),
                                           ffn1_w_in[l], ffn1_w_out[l])
        ctx = ctx + 0.5 * m_ctx[:, 2] * swiglu(modulate(rms_norm(ctx, norm1_g[l]), m_ctx[:, 0], m_ctx[:, 1]),
                                               ffn1_w_in[l], ffn1_w_out[l])

        hx = modulate(rms_norm(x, norm2_g[l]), m_lat[:, 3], m_lat[:, 4])
        hc = modulate(rms_norm(ctx, norm2_g[l]), m_ctx[:, 3], m_ctx[:, 4])
        rq, rk, rv, rg, qn, qr, kn, kr, v = project(
            hx, mix_w_in[l], mla_q_norm_g[l], mla_w_uq[l], mla_kv_norm_g[l], mla_w_ukv[l])
        crq, crk, crv, crg, cqn, cqr, ckn, ckr, cv = project(
            hc, mix_w_in[l], mla_q_norm_g[l], mla_w_uq[l], mla_kv_norm_g[l], mla_w_ukv[l])

        lg_f = jax.nn.log_sigmoid(ret_decay_fwd[l].astype(jnp.float32))
        lg_b = jax.nn.log_sigmoid(ret_decay_bwd[l].astype(jnp.float32))
        y_ctx, s_f, s_b = bidir_retention(heads(crq), heads(crk), heads(crv), lg_f, lg_b, s_zero, s_zero)
        rq = rotate(rq, *ret_tab)
        rk = rotate(rk, *ret_tab)
        y_lat, _, _ = bidir_retention(heads(rq), heads(rk), heads(rv), lg_f, lg_b, s_f, s_b)
        ret_out = retention_output(y_lat, rg)

        q_lat = jnp.concatenate([qn, axial_rope(qr, row_tab, col_tab)], axis=-1)
        k_lat = mla_keys(kn, axial_rope(kr, row_tab, col_tab))
        k_ctx = mla_keys(ckn, ckr)
        k_all = jnp.concatenate([k_lat, k_ctx], axis=1)
        v_all = jnp.concatenate([v, cv], axis=1)
        mla_out = block_attention(q_lat, k_all, v_all)

        mix = jnp.concatenate([ret_out.astype(x.dtype), mla_out.astype(x.dtype)], axis=-1) @ mix_w_out[l]
        x = x + m_lat[:, 5] * mix
        if not is_last:
            ctx_ret = retention_output(y_ctx, crg)
            ctx_mla = block_attention(jnp.concatenate([cqn, cqr], axis=-1), k_ctx, cv)
            ctx_mix = jnp.concatenate([ctx_ret.astype(ctx.dtype), ctx_mla.astype(ctx.dtype)], axis=-1) @ mix_w_out[l]
            ctx = ctx + m_ctx[:, 5] * ctx_mix

        x = x + 0.5 * m_lat[:, 8] * swiglu(modulate(rms_norm(x, norm3_g[l]), m_lat[:, 6], m_lat[:, 7]),
                                           ffn2_w_in[l], ffn2_w_out[l])
        if not is_last:
            ctx = ctx + 0.5 * m_ctx[:, 8] * swiglu(modulate(rms_norm(ctx, norm3_g[l]), m_ctx[:, 6], m_ctx[:, 7]),
                                                   ffn2_w_in[l], ffn2_w_out[l])

    return rms_norm(x, final_norm_g)
```

```python
import functools
import math

import jax
import jax.numpy as jnp
from jax import lax
from jax.experimental import pallas as pl
from jax.experimental.pallas import tpu as pltpu

GRID_W = 64
RET_DK = 64
RET_DV = 128
RET_ROPE_BASE = 10000.0
MLA_Q_RANK = 512
MLA_KV_RANK = 256
MLA_NOPE = 128
MLA_ROPE = 64
MLA_V = 128
AXIAL_BASE = 10000.0
RMS_EPS = 1e-6
GN_EPS = 1e-5

LANES = 128
MLA_QK_PAD = 2 * LANES
VMEM_LIMIT = 56 * 1024 * 1024

F32 = jnp.float32
BF16 = jnp.bfloat16


def _cparams(sem):
    return pltpu.CompilerParams(dimension_semantics=sem, vmem_limit_bytes=VMEM_LIMIT)


def _const_spec(shape):
    nd = len(shape)
    return pl.BlockSpec(shape, lambda *_: (0,) * nd, pipeline_mode=pl.Buffered(1))


def _silu(x):
    return x / (1.0 + jnp.exp(-x))


def _rms(x, g):
    return x * lax.rsqrt(jnp.mean(x * x, axis=-1, keepdims=True) + RMS_EPS) * g


def _ada_kernel(cv_ref, w_ref, b_ref, o_ref):
    s = _silu(cv_ref[...])
    o_ref[...] = jnp.dot(s, w_ref[...], preferred_element_type=F32,
                         precision=lax.Precision.HIGHEST) + b_ref[...]


def _adaln(cvec, ada_w, ada_b, tn_max=1024):
    m, d = cvec.shape
    n = ada_w.shape[1]
    tn = max(t for t in range(LANES, tn_max + 1, LANES) if n % t == 0)
    return pl.pallas_call(
        _ada_kernel,
        grid=(n // tn,),
        in_specs=[pl.BlockSpec((m, d), lambda j: (0, 0)),
                  pl.BlockSpec((d, tn), lambda j: (0, j)),
                  pl.BlockSpec((1, tn), lambda j: (0, j))],
        out_specs=pl.BlockSpec((m, tn), lambda j: (0, j)),
        out_shape=jax.ShapeDtypeStruct((m, n), F32),
        compiler_params=_cparams(("arbitrary",)),
        name="adaln",
    )(cvec, ada_w, ada_b.reshape(1, n))


def _ffn_kernel(x_ref, mod_ref, g_ref, wg_ref, wu_ref, wo_ref, *rest, final):
    if final:
        fg_ref, o_ref, xn_ref, acc_ref = rest
    else:
        o_ref, xn_ref, acc_ref = rest
    j = pl.program_id(1)
    nj = pl.num_programs(1)

    @pl.when(j == 0)
    def _():
        y = _rms(x_ref[...], g_ref[...])
        xn_ref[...] = (y * (1.0 + mod_ref[0, 1:2, :]) + mod_ref[0, 0:1, :]).astype(BF16)

    xn = xn_ref[...]
    hg = jnp.dot(xn, wg_ref[...], preferred_element_type=F32)
    hu = jnp.dot(xn, wu_ref[...], preferred_element_type=F32)
    act = (_silu(hg) * hu).astype(BF16)
    contrib = jnp.dot(act, wo_ref[...], preferred_element_type=F32)

    @pl.when(j == 0)
    def _():
        acc_ref[...] = contrib

    @pl.when(j > 0)
    def _():
        acc_ref[...] += contrib

    @pl.when(j == nj - 1)
    def _():
        out = x_ref[...] + (0.5 * mod_ref[0, 2:3, :]) * acc_ref[...]
        if final:
            out = _rms(out, fg_ref[...])
        o_ref[...] = out


def _ffn(x, mod, g, w_in, w_out, rows_per_mod, final_g=None, tm=512, tf=512):
    r, d = x.shape
    f = w_out.shape[0]
    tm = min(tm, r)
    nf = f // tf
    bpm = rows_per_mod // tm
    final = final_g is not None
    in_specs = [
        pl.BlockSpec((tm, d), lambda i, j: (i, 0)),
        pl.BlockSpec((1, 3, d), lambda i, j: (i // bpm, 0, 0)),
        pl.BlockSpec((1, d), lambda i, j: (0, 0)),
        pl.BlockSpec((d, tf), lambda i, j: (0, j)),
        pl.BlockSpec((d, tf), lambda i, j: (0, j + nf)),
        pl.BlockSpec((tf, d), lambda i, j: (j, 0)),
    ]
    args = [x, mod, g.reshape(1, d), w_in, w_in, w_out]
    if final:
        in_specs.append(pl.BlockSpec((1, d), lambda i, j: (0, 0)))
        args.append(final_g.reshape(1, d))
    return pl.pallas_call(
        functools.partial(_ffn_kernel, final=final),
        grid=(r // tm, nf),
        in_specs=in_specs,
        out_specs=pl.BlockSpec((tm, d), lambda i, j: (i, 0)),
        out_shape=jax.ShapeDtypeStruct((r, d), F32),
        scratch_shapes=[pltpu.VMEM((tm, d), BF16), pltpu.VMEM((tm, d), F32)],
        compiler_params=_cparams(("parallel", "arbitrary")),
        name="ffn_final" if final else "ffn",
    )(*args)


def _swap_halves(x, half):
    lane = lax.broadcasted_iota(jnp.int32, x.shape, 1)
    first = (lane % (2 * half)) < half
    return jnp.where(first, pltpu.roll(x, LANES - half, 1), pltpu.roll(x, half, 1))


def _rope(x, cos, sin, half):
    return x * cos + _swap_halves(x, half) * sin


def _proj_kernel(x_ref, mod_ref, g_ref, w_ref, qg_ref, wuq_ref, kvg_ref, wukv_ref, *rest,
                 heads, latent):
    if latent:
        (rcos_ref, rsin_ref, acos_ref, asin_ref,
         rq_ref, rk_ref, rv_ref, rg_ref, q_ref, k_ref, v_ref) = rest
    else:
        rk_ref, rv_ref, k_ref, v_ref = rest
    hk = heads * RET_DK
    hv = heads * RET_DV
    o_rk, o_rv, o_rg = hk, 2 * hk, 2 * hk + hv
    o_cq = o_rg + hv
    o_ckv = o_cq + MLA_Q_RANK
    o_kr = o_ckv + MLA_KV_RANK

    y = _rms(x_ref[...], g_ref[...])
    h = (y * (1.0 + mod_ref[0, 1:2, :]) + mod_ref[0, 0:1, :]).astype(BF16)
    p = jnp.dot(h, w_ref[...], preferred_element_type=F32)

    for gidx in range(hk // LANES):
        sl = slice(gidx * LANES, (gidx + 1) * LANES)
        kk = p[:, o_rk + gidx * LANES:o_rk + (gidx + 1) * LANES] * (RET_DK ** -0.5)
        if latent:
            kk = _rope(kk, rcos_ref[...], rsin_ref[...], RET_DK // 2)
            qq = _rope(p[:, sl], rcos_ref[...], rsin_ref[...], RET_DK // 2)
        for t in range(LANES // RET_DK):
            hd = gidx * (LANES // RET_DK) + t
            rk_ref[0, hd] = kk[:, t * RET_DK:(t + 1) * RET_DK].astype(BF16)
            if latent:
                rq_ref[0, hd] = qq[:, t * RET_DK:(t + 1) * RET_DK].astype(BF16)
    for hd in range(heads):
        rv_ref[0, hd] = p[:, o_rv + hd * RET_DV:o_rv + (hd + 1) * RET_DV].astype(BF16)
    if latent:
        rg_ref[0] = p[:, o_rg:o_rg + hv].astype(BF16)

    kvn = _rms(p[:, o_ckv:o_ckv + MLA_KV_RANK], kvg_ref[...]).astype(BF16)
    kv = jnp.dot(kvn, wukv_ref[...], preferred_element_type=F32)
    kr = p[:, o_kr:o_kr + LANES]
    if latent:
        kr = _rope(kr, acos_ref[...], asin_ref[...], MLA_ROPE // 4)
    kr = kr.astype(BF16)
    for hd in range(heads):
        base = hd * (MLA_NOPE + MLA_V)
        k_ref[0, hd, :, 0:MLA_NOPE] = kv[:, base:base + MLA_NOPE].astype(BF16)
        k_ref[0, hd, :, MLA_NOPE:MLA_QK_PAD] = kr
        v_ref[0, hd] = kv[:, base + MLA_NOPE:base + MLA_NOPE + MLA_V].astype(BF16)
    if latent:
        qn = _rms(p[:, o_cq:o_cq + MLA_Q_RANK], qg_ref[...]).astype(BF16)
        q = jnp.dot(qn, wuq_ref[...], preferred_element_type=F32)
        scale = (MLA_NOPE + MLA_ROPE) ** -0.5
        for hd in range(heads):
            base = hd * MLA_QK_PAD
            q_ref[0, hd, :, 0:MLA_NOPE] = (q[:, base:base + MLA_NOPE] * scale).astype(BF16)
            qr = _rope(q[:, base + MLA_NOPE:base + MLA_QK_PAD],
                       acos_ref[...], asin_ref[...], MLA_ROPE // 4)
            q_ref[0, hd, :, MLA_NOPE:MLA_QK_PAD] = (qr * scale).astype(BF16)


def _proj(x, mod, g, w_in_p, qg, wuq_p, kvg, wukv, heads, batch, tables=None, tm=256):
    r, d = x.shape
    n = r // batch
    tm = min(tm, n)
    nb = n // tm
    latent = tables is not None
    row = lambda i: (i, 0)
    cst2 = lambda i: (0, 0)
    in_specs = [
        pl.BlockSpec((tm, d), row),
        pl.BlockSpec((1, 3, d), lambda i: (i // nb if latent else 0, 0, 0)),
        pl.BlockSpec((1, d), cst2),
        _const_spec(w_in_p.shape),
        pl.BlockSpec((1, MLA_Q_RANK), cst2),
        _const_spec(wuq_p.shape),
        pl.BlockSpec((1, MLA_KV_RANK), cst2),
        _const_spec(wukv.shape),
    ]
    args = [x, mod, g.reshape(1, d), w_in_p, qg.reshape(1, -1), wuq_p, kvg.reshape(1, -1), wukv]
    hmaj = lambda w: pl.BlockSpec((1, heads, tm, w), lambda i: (i // nb, 0, i % nb, 0))
    hshape = lambda w: jax.ShapeDtypeStruct((batch, heads, n, w), BF16)
    if latent:
        in_specs += [pl.BlockSpec((tm, LANES), lambda i: (i % nb, 0))] * 4
        args += list(tables)
        out_specs = [hmaj(RET_DK), hmaj(RET_DK), hmaj(RET_DV),
                     pl.BlockSpec((1, tm, heads * RET_DV), lambda i: (i // nb, i % nb, 0)),
                     hmaj(MLA_QK_PAD), hmaj(MLA_QK_PAD), hmaj(MLA_V)]
        out_shape = [hshape(RET_DK), hshape(RET_DK), hshape(RET_DV),
                     jax.ShapeDtypeStruct((batch, n, heads * RET_DV), BF16),
                     hshape(MLA_QK_PAD), hshape(MLA_QK_PAD), hshape(MLA_V)]
    else:
        out_specs = [hmaj(RET_DK), hmaj(RET_DV), hmaj(MLA_QK_PAD), hmaj(MLA_V)]
        out_shape = [hshape(RET_DK), hshape(RET_DV), hshape(MLA_QK_PAD), hshape(MLA_V)]
    return pl.pallas_call(
        functools.partial(_proj_kernel, heads=heads, latent=latent),
        grid=(r // tm,),
        in_specs=in_specs,
        out_specs=out_specs,
        out_shape=out_shape,
        compiler_params=_cparams(("parallel",)),
        name="proj_lat" if latent else "proj_ctx",
    )(*args)


def _log_sigmoid(x):
    return jnp.minimum(x, 0.0) - jnp.log1p(jnp.exp(-jnp.abs(x)))


def _ret_kernel(dec_ref, q_ref, k_ref, v_ref, g_ref, ck_ref, cv_ref, o_ref, kv_ref, st_ref,
                *, chunk):
    hd = pl.program_id(1)
    n = q_ref.shape[2]
    nctx = ck_ref.shape[2]
    nc = n // chunk
    dk = RET_DK
    lgf = _log_sigmoid(jnp.full((1, 1), dec_ref[0, hd], F32))
    lgb = _log_sigmoid(jnp.full((1, 1), dec_ref[1, hd], F32))

    def col(m):
        return lax.broadcasted_iota(jnp.int32, (m, 1), 0).astype(F32)

    def weighted_kv(k, v, zf, zb):
        kf = k.astype(F32)
        kz = jnp.concatenate([kf * zf, kf * zb], axis=-1).astype(BF16)
        return lax.dot_general(kz, v, (((0,), (0,)), ((), ())), preferred_element_type=F32)

    ic = col(nctx)
    s0 = weighted_kv(ck_ref[0, 0], cv_ref[0, 0],
                     jnp.exp(lgf * (nctx - 1.0 - ic)), jnp.exp(lgb * ic))

    i = col(chunk)
    zeta_f = jnp.exp(lgf * (chunk - 1.0 - i))
    zeta_b = jnp.exp(lgb * i)
    xi_f = jnp.exp(lgf * (i + 1.0))
    xi_b = jnp.exp(lgb * (chunk - i))
    dec_f = jnp.exp(lgf * float(chunk))
    dec_b = jnp.exp(lgb * float(chunk))
    diff = i - lax.broadcasted_iota(jnp.int32, (1, chunk), 1).astype(F32)
    dmat = jnp.where(diff >= 0.0, jnp.exp(lgf * jnp.maximum(diff, 0.0)),
                     jnp.exp(lgb * jnp.maximum(-diff, 0.0)))

    def rows(c):
        return pl.ds(pl.multiple_of(c * chunk, chunk), chunk)

    def p1(c, carry):
        kv_ref[c] = weighted_kv(k_ref[0, 0, rows(c), :], v_ref[0, 0, rows(c), :], zeta_f, zeta_b)
        return carry
    lax.fori_loop(0, nc, p1, 0)

    def p2f(c, s):
        st_ref[c, 0:dk, :] = s.astype(BF16)
        return dec_f * s + kv_ref[c, 0:dk, :]
    lax.fori_loop(0, nc, p2f, s0[0:dk])

    def p2b(t, s):
        c = nc - 1 - t
        st_ref[c, dk:2 * dk, :] = s.astype(BF16)
        return dec_b * s + kv_ref[c, dk:2 * dk, :]
    lax.fori_loop(0, nc, p2b, s0[dk:2 * dk])

    def p3(c, carry):
        q = q_ref[0, 0, rows(c), :]
        k = k_ref[0, 0, rows(c), :]
        v = v_ref[0, 0, rows(c), :]
        a = lax.dot_general(q, k, (((1,), (1,)), ((), ())), preferred_element_type=F32)
        y = jnp.dot((a * dmat).astype(BF16), v, preferred_element_type=F32)
        y += xi_f * jnp.dot(q, st_ref[c, 0:dk, :], preferred_element_type=F32)
        y += xi_b * jnp.dot(q, st_ref[c, dk:2 * dk, :], preferred_element_type=F32)
        mu = jnp.mean(y, axis=-1, keepdims=True)
        yc = y - mu
        var = jnp.mean(yc * yc, axis=-1, keepdims=True)
        yn = yc * lax.rsqrt(var + GN_EPS)
        o_ref[0, rows(c), :] = (_silu(g_ref[0, rows(c), :].astype(F32)) * yn).astype(BF16)
        return carry
    lax.fori_loop(0, nc, p3, 0)


def _retention(dec, rq, rk, rv, rg, crk, crv, chunk=256):
    b, h, n, _ = rq.shape
    nctx = crk.shape[2]
    chunk = min(chunk, n)
    hm = lambda w, rows: pl.BlockSpec((1, 1, rows, w), lambda bi, hi: (bi, hi, 0, 0))
    return pl.pallas_call(
        functools.partial(_ret_kernel, chunk=chunk),
        grid=(b, h),
        in_specs=[pl.BlockSpec(memory_space=pltpu.SMEM),
                  hm(RET_DK, n), hm(RET_DK, n), hm(RET_DV, n),
                  pl.BlockSpec((1, n, RET_DV), lambda bi, hi: (bi, 0, hi)),
                  hm(RET_DK, nctx), hm(RET_DV, nctx)],
        out_specs=pl.BlockSpec((1, n, RET_DV), lambda bi, hi: (bi, 0, hi)),
        out_shape=jax.ShapeDtypeStruct((b, n, h * RET_DV), BF16),
        scratch_shapes=[pltpu.VMEM((n // chunk, 2 * RET_DK, RET_DV), F32),
                        pltpu.VMEM((n // chunk, 2 * RET_DK, RET_DV), BF16)],
        compiler_params=_cparams(("parallel", "parallel")),
        name="retention",
    )(dec, rq, rk, rv, rg, crk, crv)


def _mla_kernel(q_ref, k_ref, v_ref, ck_ref, cv_ref, o_ref, m_ref, l_ref, acc_ref):
    j = pl.program_id(3)
    nj = pl.num_programs(3)
    q = q_ref[0, 0]
    nt = (((1,), (1,)), ((), ()))

    @pl.when(j == 0)
    def _():
        s = lax.dot_general(q, ck_ref[0, 0], nt, preferred_element_type=F32)
        m = jnp.max(s, axis=-1, keepdims=True)
        p = jnp.exp(s - m)
        m_ref[...] = m
        l_ref[...] = jnp.sum(p, axis=-1, keepdims=True)
        acc_ref[...] = jnp.dot(p.astype(BF16), cv_ref[0, 0], preferred_element_type=F32)

    s = lax.dot_general(q, k_ref[0, 0], nt, preferred_element_type=F32)
    m_old = m_ref[...]
    m_new = jnp.maximum(m_old, jnp.max(s, axis=-1, keepdims=True))
    alpha = jnp.exp(m_old - m_new)
    p = jnp.exp(s - m_new)
    l_ref[...] = alpha * l_ref[...] + jnp.sum(p, axis=-1, keepdims=True)
    acc_ref[...] = alpha * acc_ref[...] + jnp.dot(p.astype(BF16), v_ref[0, 0],
                                                  preferred_element_type=F32)
    m_ref[...] = m_new

    @pl.when(j == nj - 1)
    def _():
        o_ref[0] = (acc_ref[...] / l_ref[...]).astype(BF16)


def _mla(q, k, v, ck, cv, tq=512, tk=1024):
    b, h, n, w = q.shape
    nctx = ck.shape[2]
    tq = min(tq, n)
    tk = min(tk, n)
    return pl.pallas_call(
        _mla_kernel,
        grid=(b, h, n // tq, n // tk),
        in_specs=[pl.BlockSpec((1, 1, tq, w), lambda bi, hi, i, j: (bi, hi, i, 0)),
                  pl.BlockSpec((1, 1, tk, w), lambda bi, hi, i, j: (bi, hi, j, 0)),
                  pl.BlockSpec((1, 1, tk, MLA_V), lambda bi, hi, i, j: (bi, hi, j, 0)),
                  pl.BlockSpec((1, 1, nctx, w), lambda bi, hi, i, j: (bi, hi, 0, 0)),
                  pl.BlockSpec((1, 1, nctx, MLA_V), lambda bi, hi, i, j: (bi, hi, 0, 0))],
        out_specs=pl.BlockSpec((1, tq, MLA_V), lambda bi, hi, i, j: (bi, i, hi)),
        out_shape=jax.ShapeDtypeStruct((b, n, h * MLA_V), BF16),
        scratch_shapes=[pltpu.VMEM((tq, 1), F32), pltpu.VMEM((tq, 1), F32),
                        pltpu.VMEM((tq, MLA_V), F32)],
        compiler_params=_cparams(("parallel", "parallel", "parallel", "arbitrary")),
        name="mla_attn",
    )(q, k, v, ck, cv)


def _mixout_kernel(x_ref, mod_ref, ret_ref, mla_ref, w_ref, o_ref):
    hr = ret_ref.shape[1]
    mix = jnp.dot(ret_ref[...], w_ref[0:hr, :], preferred_element_type=F32)
    mix += jnp.dot(mla_ref[...], w_ref[hr:, :], preferred_element_type=F32)
    o_ref[...] = x_ref[...] + mod_ref[0, 2:3, :] * mix


def _mixout(x, mod, ret, mla, w, rows_per_mod, tm=512):
    r, d = x.shape
    tm = min(tm, r)
    bpm = rows_per_mod // tm
    return pl.pallas_call(
        _mixout_kernel,
        grid=(r // tm,),
        in_specs=[pl.BlockSpec((tm, d), lambda i: (i, 0)),
                  pl.BlockSpec((1, 3, d), lambda i: (i // bpm, 0, 0)),
                  pl.BlockSpec((tm, ret.shape[1]), lambda i: (i, 0)),
                  pl.BlockSpec((tm, mla.shape[1]), lambda i: (i, 0)),
                  _const_spec(w.shape)],
        out_specs=pl.BlockSpec((tm, d), lambda i: (i, 0)),
        out_shape=jax.ShapeDtypeStruct((r, d), F32),
        compiler_params=_cparams(("parallel",)),
        name="mixout",
    )(x, mod, ret, mla, w)


def _rope_tables(n):
    pos = jnp.arange(n, dtype=F32)
    inv_r = RET_ROPE_BASE ** (-jnp.arange(0, RET_DK, 2, dtype=F32) / RET_DK)
    ang = pos[:, None] * inv_r[None, :]
    c, s = jnp.cos(ang), jnp.sin(ang)
    rcos = jnp.tile(jnp.concatenate([c, c], -1), (1, LANES // RET_DK))
    rsin = jnp.tile(jnp.concatenate([-s, s], -1), (1, LANES // RET_DK))
    ax = MLA_ROPE // 2
    inv_a = AXIAL_BASE ** (-jnp.arange(0, ax, 2, dtype=F32) / ax)
    pr = (jnp.arange(n) // GRID_W).astype(F32)[:, None] * inv_a[None, :]
    pc = (jnp.arange(n) % GRID_W).astype(F32)[:, None] * inv_a[None, :]
    cr, sr, cc, sc = jnp.cos(pr), jnp.sin(pr), jnp.cos(pc), jnp.sin(pc)
    pad1 = jnp.ones((n, LANES - MLA_ROPE), F32)
    pad0 = jnp.zeros((n, LANES - MLA_ROPE), F32)
    acos = jnp.concatenate([cr, cr, cc, cc, pad1], -1)
    asin = jnp.concatenate([-sr, sr, -sc, sc, pad0], -1)
    return rcos, rsin, acos, asin


def kernel(x, c, ctx, c_ctx, ada_w, ada_b, norm1_g, ffn1_w_in, ffn1_w_out, norm2_g, mix_w_in,
           ret_decay_fwd, ret_decay_bwd, mla_q_norm_g, mla_w_uq, mla_kv_norm_g, mla_w_ukv,
           mix_w_out, norm3_g, ffn2_w_in, ffn2_w_out, final_norm_g):
    b, n, d = x.shape
    nctx = ctx.shape[1]
    depth = ada_w.shape[0]
    heads = ret_decay_fwd.shape[1]
    assert depth == 1, "single-layer block"
    assert mla_w_uq.shape[2] == heads * (MLA_NOPE + MLA_ROPE)
    l = 0

    mrows = 8
    cvec = jnp.concatenate([c, c_ctx[None, :], jnp.zeros((mrows - b - 1, d), F32)], 0)
    mods = _adaln(cvec, ada_w[l], ada_b[l]).reshape(mrows, 9, d)
    m_lat, m_ctx = mods[:b], mods[b:b + 1]

    w1i, w1o = ffn1_w_in[l].astype(BF16), ffn1_w_out[l].astype(BF16)
    w2i, w2o = ffn2_w_in[l].astype(BF16), ffn2_w_out[l].astype(BF16)
    wmo = mix_w_out[l].astype(BF16)
    wmi = jnp.pad(mix_w_in[l], ((0, 0), (0, LANES - MLA_ROPE))).astype(BF16)
    wuq = mla_w_uq[l].reshape(MLA_Q_RANK, heads, MLA_NOPE + MLA_ROPE)
    wuq = jnp.pad(wuq, ((0, 0), (0, 0), (0, MLA_QK_PAD - MLA_NOPE - MLA_ROPE)))
    wuq = wuq.reshape(MLA_Q_RANK, heads * MLA_QK_PAD).astype(BF16)
    wukv = mla_w_ukv[l].astype(BF16)

    x2 = x.reshape(b * n, d)
    c2 = ctx.reshape(b * nctx, d)

    x2 = _ffn(x2, m_lat[:, 0:3], norm1_g[l], w1i, w1o, rows_per_mod=n)
    c2 = _ffn(c2, m_ctx[:, 0:3], norm1_g[l], w1i, w1o, rows_per_mod=b * nctx)

    tables = _rope_tables(n)
    rq, rk, rv, rg, q, k, v = _proj(x2, m_lat[:, 3:6], norm2_g[l], wmi, mla_q_norm_g[l], wuq,
                                    mla_kv_norm_g[l], wukv, heads, b, tables)
    crk, crv, ck, cv = _proj(c2, m_ctx[:, 3:6], norm2_g[l], wmi, mla_q_norm_g[l], wuq,
                             mla_kv_norm_g[l], wukv, heads, b)
    dec = jnp.stack([ret_decay_fwd[l], ret_decay_bwd[l]]).astype(F32)
    ret = _retention(dec, rq, rk, rv, rg, crk, crv)
    mla = _mla(q, k, v, ck, cv)
    x2 = _mixout(x2, m_lat[:, 3:6], ret.reshape(b * n, -1), mla.reshape(b * n, -1), wmo,
                 rows_per_mod=n)

    x2 = _ffn(x2, m_lat[:, 6:9], norm3_g[l], w2i, w2o, rows_per_mod=n, final_g=final_norm_g)
    return x2.reshape(b, n, d)
```

```python
import functools
import math

import jax
import jax.numpy as jnp
from jax import lax
from jax.experimental import pallas as pl
from jax.experimental.pallas import tpu as pltpu

GRID_W = 64
RET_DK = 64
RET_DV = 128
RET_ROPE_BASE = 10000.0
MLA_Q_RANK = 512
MLA_KV_RANK = 256
MLA_NOPE = 128
MLA_ROPE = 64
MLA_V = 128
AXIAL_BASE = 10000.0
RMS_EPS = 1e-6
GN_EPS = 1e-5

LANES = 128
MLA_QK_PAD = 2 * LANES
VMEM_LIMIT = 56 * 1024 * 1024

F32 = jnp.float32
BF16 = jnp.bfloat16


def _cparams(sem):
    return pltpu.CompilerParams(dimension_semantics=sem, vmem_limit_bytes=VMEM_LIMIT)


def _const_spec(shape):
    nd = len(shape)
    return pl.BlockSpec(shape, lambda *_: (0,) * nd, pipeline_mode=pl.Buffered(1))


def _silu(x):
    return x / (1.0 + jnp.exp(-x))


def _rms(x, g):
    return x * lax.rsqrt(jnp.mean(x * x, axis=-1, keepdims=True) + RMS_EPS) * g


def _ada_kernel(cv_ref, w_ref, b_ref, o_ref):
    s = _silu(cv_ref[...])
    o_ref[...] = jnp.dot(s, w_ref[...], preferred_element_type=F32,
                         precision=lax.Precision.HIGHEST) + b_ref[...]


def _adaln(cvec, ada_w, ada_b, tn_max=1024):
    m, d = cvec.shape
    n = ada_w.shape[1]
    tn = max(t for t in range(LANES, tn_max + 1, LANES) if n % t == 0)
    return pl.pallas_call(
        _ada_kernel,
        grid=(n // tn,),
        in_specs=[pl.BlockSpec((m, d), lambda j: (0, 0)),
                  pl.BlockSpec((d, tn), lambda j: (0, j)),
                  pl.BlockSpec((1, tn), lambda j: (0, j))],
        out_specs=pl.BlockSpec((m, tn), lambda j: (0, j)),
        out_shape=jax.ShapeDtypeStruct((m, n), F32),
        compiler_params=_cparams(("arbitrary",)),
        name="adaln",
    )(cvec, ada_w, ada_b.reshape(1, n))


def _ffn_kernel(x_ref, mod_ref, g_ref, wg_ref, wu_ref, wo_ref, *rest, final):
    if final:
        fg_ref, o_ref, xn_ref, acc_ref = rest
    else:
        o_ref, xn_ref, acc_ref = rest
    j = pl.program_id(1)
    nj = pl.num_programs(1)

    @pl.when(j == 0)
    def _():
        y = _rms(x_ref[...], g_ref[...])
        xn_ref[...] = (y * (1.0 + mod_ref[0, 1:2, :]) + mod_ref[0, 0:1, :]).astype(BF16)
        acc_ref[...] = jnp.zeros_like(acc_ref)

    xn = xn_ref[...]
    hg = jnp.dot(xn, wg_ref[...], preferred_element_type=F32)
    hu = jnp.dot(xn, wu_ref[...], preferred_element_type=F32)
    act = (_silu(hg) * hu).astype(BF16)
    acc_ref[...] += jnp.dot(act, wo_ref[...], preferred_element_type=F32)

    @pl.when(j == nj - 1)
    def _():
        out = x_ref[...] + (0.5 * mod_ref[0, 2:3, :]) * acc_ref[...]
        if final:
            out = _rms(out, fg_ref[...])
        o_ref[...] = out


def _ffn(x, mod, g, w_in, w_out, rows_per_mod, final_g=None, tm=512, tf=512):
    r, d = x.shape
    f = w_out.shape[0]
    tm = min(tm, r)
    nf = f // tf
    bpm = rows_per_mod // tm
    final = final_g is not None
    in_specs = [
        pl.BlockSpec((tm, d), lambda i, j: (i, 0)),
        pl.BlockSpec((1, 3, d), lambda i, j: (i // bpm, 0, 0)),
        pl.BlockSpec((1, d), lambda i, j: (0, 0)),
        pl.BlockSpec((d, tf), lambda i, j: (0, j)),
        pl.BlockSpec((d, tf), lambda i, j: (0, j + nf)),
        pl.BlockSpec((tf, d), lambda i, j: (j, 0)),
    ]
    args = [x, mod, g.reshape(1, d), w_in, w_in, w_out]
    if final:
        in_specs.append(pl.BlockSpec((1, d), lambda i, j: (0, 0)))
        args.append(final_g.reshape(1, d))
    return pl.pallas_call(
        functools.partial(_ffn_kernel, final=final),
        grid=(r // tm, nf),
        in_specs=in_specs,
        out_specs=pl.BlockSpec((tm, d), lambda i, j: (i, 0)),
        out_shape=jax.ShapeDtypeStruct((r, d), F32),
        scratch_shapes=[pltpu.VMEM((tm, d), BF16), pltpu.VMEM((tm, d), F32)],
        compiler_params=_cparams(("parallel", "arbitrary")),
        name="ffn_final" if final else "ffn",
    )(*args)


def _swap_halves(x, half):
    lane = lax.broadcasted_iota(jnp.int32, x.shape, 1)
    first = (lane % (2 * half)) < half
    return jnp.where(first, pltpu.roll(x, LANES - half, 1), pltpu.roll(x, half, 1))


def _rope(x, cos, sin, half):
    return x * cos + _swap_halves(x, half) * sin


def _proj_kernel(x_ref, mod_ref, g_ref, w_ref, qg_ref, wuq_ref, kvg_ref, wukv_ref, *rest,
                 heads, latent):
    if latent:
        (rcos_ref, rsin_ref, acos_ref, asin_ref,
         rq_ref, rk_ref, rv_ref, rg_ref, q_ref, k_ref, v_ref) = rest
    else:
        rk_ref, rv_ref, k_ref, v_ref = rest
    hk = heads * RET_DK
    hv = heads * RET_DV
    o_rk, o_rv, o_rg = hk, 2 * hk, 2 * hk + hv
    o_cq = o_rg + hv
    o_ckv = o_cq + MLA_Q_RANK
    o_kr = o_ckv + MLA_KV_RANK

    y = _rms(x_ref[...], g_ref[...])
    h = (y * (1.0 + mod_ref[0, 1:2, :]) + mod_ref[0, 0:1, :]).astype(BF16)
    p = jnp.dot(h, w_ref[...], preferred_element_type=F32)

    for gidx in range(hk // LANES):
        sl = slice(gidx * LANES, (gidx + 1) * LANES)
        kk = p[:, o_rk + gidx * LANES:o_rk + (gidx + 1) * LANES] * (RET_DK ** -0.5)
        if latent:
            kk = _rope(kk, rcos_ref[...], rsin_ref[...], RET_DK // 2)
            qq = _rope(p[:, sl], rcos_ref[...], rsin_ref[...], RET_DK // 2)
        for t in range(LANES // RET_DK):
            hd = gidx * (LANES // RET_DK) + t
            rk_ref[0, hd] = kk[:, t * RET_DK:(t + 1) * RET_DK].astype(BF16)
            if latent:
                rq_ref[0, hd] = qq[:, t * RET_DK:(t + 1) * RET_DK].astype(BF16)
    for hd in range(heads):
        rv_ref[0, hd] = p[:, o_rv + hd * RET_DV:o_rv + (hd + 1) * RET_DV].astype(BF16)
    if latent:
        rg_ref[0] = p[:, o_rg:o_rg + hv].astype(BF16)

    kvn = _rms(p[:, o_ckv:o_ckv + MLA_KV_RANK], kvg_ref[...]).astype(BF16)
    kv = jnp.dot(kvn, wukv_ref[...], preferred_element_type=F32)
    kr = p[:, o_kr:o_kr + LANES]
    if latent:
        kr = _rope(kr, acos_ref[...], asin_ref[...], MLA_ROPE // 4)
    kr = kr.astype(BF16)
    for hd in range(heads):
        base = hd * (MLA_NOPE + MLA_V)
        k_ref[0, hd, :, 0:MLA_NOPE] = kv[:, base:base + MLA_NOPE].astype(BF16)
        k_ref[0, hd, :, MLA_NOPE:MLA_QK_PAD] = kr
        v_ref[0, hd] = kv[:, base + MLA_NOPE:base + MLA_NOPE + MLA_V].astype(BF16)
    if latent:
        qn = _rms(p[:, o_cq:o_cq + MLA_Q_RANK], qg_ref[...]).astype(BF16)
        q = jnp.dot(qn, wuq_ref[...], preferred_element_type=F32)
        scale = (MLA_NOPE + MLA_ROPE) ** -0.5 * math.log2(math.e)
        for hd in range(heads):
            base = hd * MLA_QK_PAD
            q_ref[0, hd, :, 0:MLA_NOPE] = (q[:, base:base + MLA_NOPE] * scale).astype(BF16)
            qr = _rope(q[:, base + MLA_NOPE:base + MLA_QK_PAD],
                       acos_ref[...], asin_ref[...], MLA_ROPE // 4)
            q_ref[0, hd, :, MLA_NOPE:MLA_QK_PAD] = (qr * scale).astype(BF16)


def _proj(x, mod, g, w_in_p, qg, wuq_p, kvg, wukv, heads, batch, tables=None, tm=256):
    r, d = x.shape
    n = r // batch
    tm = min(tm, n)
    nb = n // tm
    latent = tables is not None
    row = lambda i: (i, 0)
    cst2 = lambda i: (0, 0)
    in_specs = [
        pl.BlockSpec((tm, d), row),
        pl.BlockSpec((1, 3, d), lambda i: (i // nb if latent else 0, 0, 0)),
        pl.BlockSpec((1, d), cst2),
        _const_spec(w_in_p.shape),
        pl.BlockSpec((1, MLA_Q_RANK), cst2),
        _const_spec(wuq_p.shape),
        pl.BlockSpec((1, MLA_KV_RANK), cst2),
        _const_spec(wukv.shape),
    ]
    args = [x, mod, g.reshape(1, d), w_in_p, qg.reshape(1, -1), wuq_p, kvg.reshape(1, -1), wukv]
    hmaj = lambda w: pl.BlockSpec((1, heads, tm, w), lambda i: (i // nb, 0, i % nb, 0))
    hshape = lambda w: jax.ShapeDtypeStruct((batch, heads, n, w), BF16)
    if latent:
        in_specs += [pl.BlockSpec((tm, LANES), lambda i: (i % nb, 0))] * 4
        args += list(tables)
        out_specs = [hmaj(RET_DK), hmaj(RET_DK), hmaj(RET_DV),
                     pl.BlockSpec((1, tm, heads * RET_DV), lambda i: (i // nb, i % nb, 0)),
                     hmaj(MLA_QK_PAD), hmaj(MLA_QK_PAD), hmaj(MLA_V)]
        out_shape = [hshape(RET_DK), hshape(RET_DK), hshape(RET_DV),
                     jax.ShapeDtypeStruct((batch, n, heads * RET_DV), BF16),
                     hshape(MLA_QK_PAD), hshape(MLA_QK_PAD), hshape(MLA_V)]
    else:
        out_specs = [hmaj(RET_DK), hmaj(RET_DV), hmaj(MLA_QK_PAD), hmaj(MLA_V)]
        out_shape = [hshape(RET_DK), hshape(RET_DV), hshape(MLA_QK_PAD), hshape(MLA_V)]
    return pl.pallas_call(
        functools.partial(_proj_kernel, heads=heads, latent=latent),
        grid=(r // tm,),
        in_specs=in_specs,
        out_specs=out_specs,
        out_shape=out_shape,
        compiler_params=_cparams(("parallel",)),
        name="proj_lat" if latent else "proj_ctx",
    )(*args)


def _log_sigmoid(x):
    return jnp.minimum(x, 0.0) - jnp.log1p(jnp.exp(-jnp.abs(x)))


def _ret_kernel(dec_ref, q_ref, k_ref, v_ref, g_ref, ck_ref, cv_ref, o_ref, kv_ref, st_ref,
                *, chunk, unroll):
    hd = pl.program_id(1)
    n = q_ref.shape[2]
    nctx = ck_ref.shape[2]
    nc = n // chunk
    dk = RET_DK
    lgf = _log_sigmoid(jnp.full((1, 1), dec_ref[0, hd], F32))
    lgb = _log_sigmoid(jnp.full((1, 1), dec_ref[1, hd], F32))

    def col(m):
        return lax.broadcasted_iota(jnp.int32, (m, 1), 0).astype(F32)

    def weighted_kv(k, v, zf, zb):
        kf = k.astype(F32)
        kz = jnp.concatenate([kf * zf, kf * zb], axis=-1).astype(BF16)
        return lax.dot_general(kz, v, (((0,), (0,)), ((), ())), preferred_element_type=F32)

    ic = col(nctx)
    s0 = weighted_kv(ck_ref[0, 0], cv_ref[0, 0],
                     jnp.exp(lgf * (nctx - 1.0 - ic)), jnp.exp(lgb * ic))

    i = col(chunk)
    zeta_f = jnp.exp(lgf * (chunk - 1.0 - i))
    zeta_b = jnp.exp(lgb * i)
    xi_f = jnp.exp(lgf * (i + 1.0))
    xi_b = jnp.exp(lgb * (chunk - i))
    dec_f = jnp.exp(lgf * float(chunk))
    dec_b = jnp.exp(lgb * float(chunk))
    diff = i - lax.broadcasted_iota(jnp.int32, (1, chunk), 1).astype(F32)
    dmat = jnp.where(diff >= 0.0, jnp.exp(lgf * jnp.maximum(diff, 0.0)),
                     jnp.exp(lgb * jnp.maximum(-diff, 0.0)))

    def rows(c):
        return pl.ds(pl.multiple_of(c * chunk, chunk), chunk)

    def p1(c, carry):
        kv_ref[c] = weighted_kv(k_ref[0, 0, rows(c), :], v_ref[0, 0, rows(c), :], zeta_f, zeta_b)
        return carry
    lax.fori_loop(0, nc, p1, 0, unroll=unroll)

    def p2f(c, s):
        st_ref[c, 0:dk, :] = s.astype(BF16)
        return dec_f * s + kv_ref[c, 0:dk, :]
    lax.fori_loop(0, nc, p2f, s0[0:dk])

    def p2b(t, s):
        c = nc - 1 - t
        st_ref[c, dk:2 * dk, :] = s.astype(BF16)
        return dec_b * s + kv_ref[c, dk:2 * dk, :]
    lax.fori_loop(0, nc, p2b, s0[dk:2 * dk])

    def p3(c, carry):
        q = q_ref[0, 0, rows(c), :]
        k = k_ref[0, 0, rows(c), :]
        v = v_ref[0, 0, rows(c), :]
        a = lax.dot_general(q, k, (((1,), (1,)), ((), ())), preferred_element_type=F32)
        y = jnp.dot((a * dmat).astype(BF16), v, preferred_element_type=F32)
        y += xi_f * jnp.dot(q, st_ref[c, 0:dk, :], preferred_element_type=F32)
        y += xi_b * jnp.dot(q, st_ref[c, dk:2 * dk, :], preferred_element_type=F32)
        mu = jnp.mean(y, axis=-1, keepdims=True)
        yc = y - mu
        var = jnp.mean(yc * yc, axis=-1, keepdims=True)
        yn = yc * lax.rsqrt(var + GN_EPS)
        o_ref[0, rows(c), :] = (_silu(g_ref[0, rows(c), :].astype(F32)) * yn).astype(BF16)
        return carry
    lax.fori_loop(0, nc, p3, 0, unroll=unroll)


def _retention(dec, rq, rk, rv, rg, crk, crv, chunk=256, unroll=4):
    b, h, n, _ = rq.shape
    nctx = crk.shape[2]
    chunk = min(chunk, n)
    unroll = math.gcd(unroll, n // chunk)
    hm = lambda w, rows: pl.BlockSpec((1, 1, rows, w), lambda bi, hi: (bi, hi, 0, 0))
    return pl.pallas_call(
        functools.partial(_ret_kernel, chunk=chunk, unroll=unroll),
        grid=(b, h),
        in_specs=[pl.BlockSpec(memory_space=pltpu.SMEM),
                  hm(RET_DK, n), hm(RET_DK, n), hm(RET_DV, n),
                  pl.BlockSpec((1, n, RET_DV), lambda bi, hi: (bi, 0, hi)),
                  hm(RET_DK, nctx), hm(RET_DV, nctx)],
        out_specs=pl.BlockSpec((1, n, RET_DV), lambda bi, hi: (bi, 0, hi)),
        out_shape=jax.ShapeDtypeStruct((b, n, h * RET_DV), BF16),
        scratch_shapes=[pltpu.VMEM((n // chunk, 2 * RET_DK, RET_DV), F32),
                        pltpu.VMEM((n // chunk, 2 * RET_DK, RET_DV), BF16)],
        compiler_params=_cparams(("parallel", "parallel")),
        name="retention",
    )(dec, rq, rk, rv, rg, crk, crv)


def _mla_kernel(q_ref, k_ref, v_ref, ck_ref, cv_ref, o_ref, m_ref, acc_ref, *, sub):
    j = pl.program_id(3)
    nj = pl.num_programs(3)
    nt = (((1,), (1,)), ((), ()))

    def with_ones(v):
        return jnp.concatenate([v, jnp.ones_like(v)], axis=-1)

    row_tiles = [slice(r * sub, (r + 1) * sub) for r in range(q_ref.shape[2] // sub)]

    @pl.when(j == 0)
    def _():
        s = lax.dot_general(q_ref[0, 0], ck_ref[0, 0], nt, preferred_element_type=F32)
        m = jnp.max(s, axis=-1, keepdims=True)
        p = jnp.exp2(s - m)
        m_ref[...] = jnp.broadcast_to(m, m_ref.shape)
        acc_ref[...] = jnp.dot(p.astype(BF16), with_ones(cv_ref[0, 0]),
                               preferred_element_type=F32)

    k = k_ref[0, 0]
    v1 = with_ones(v_ref[0, 0])
    for rows in row_tiles:
        s = lax.dot_general(q_ref[0, 0, rows, :], k, nt, preferred_element_type=F32)
        m_old = m_ref[rows, :]
        m_new = jnp.maximum(m_old, jnp.max(s, axis=-1, keepdims=True))
        alpha = jnp.exp2(m_old - m_new)
        p = jnp.exp2(s - jnp.tile(m_new, (1, k.shape[0] // LANES)))
        acc_ref[rows, :] = jnp.tile(alpha, (1, 2)) * acc_ref[rows, :] + jnp.dot(
            p.astype(BF16), v1, preferred_element_type=F32)
        m_ref[rows, :] = m_new

    @pl.when(j == nj - 1)
    def _():
        acc = acc_ref[...]
        o_ref[0] = (acc[:, :MLA_V] / acc[:, MLA_V:]).astype(BF16)


def _mla(q, k, v, ck, cv, tq=2048, tk=1024, sub=512):
    b, h, n, w = q.shape
    nctx = ck.shape[2]
    tq = min(tq, n)
    tk = min(tk, n)
    return pl.pallas_call(
        functools.partial(_mla_kernel, sub=min(sub, tq)),
        grid=(b, h, n // tq, n // tk),
        in_specs=[pl.BlockSpec((1, 1, tq, w), lambda bi, hi, i, j: (bi, hi, i, 0)),
                  pl.BlockSpec((1, 1, tk, w), lambda bi, hi, i, j: (bi, hi, j, 0)),
                  pl.BlockSpec((1, 1, tk, MLA_V), lambda bi, hi, i, j: (bi, hi, j, 0)),
                  pl.BlockSpec((1, 1, nctx, w), lambda bi, hi, i, j: (bi, hi, 0, 0)),
                  pl.BlockSpec((1, 1, nctx, MLA_V), lambda bi, hi, i, j: (bi, hi, 0, 0))],
        out_specs=pl.BlockSpec((1, tq, MLA_V), lambda bi, hi, i, j: (bi, i, hi)),
        out_shape=jax.ShapeDtypeStruct((b, n, h * MLA_V), BF16),
        scratch_shapes=[pltpu.VMEM((tq, LANES), F32), pltpu.VMEM((tq, 2 * MLA_V), F32)],
        compiler_params=_cparams(("parallel", "parallel", "parallel", "arbitrary")),
        name="mla_attn",
    )(q, k, v, ck, cv)


def _mixout_kernel(x_ref, mod_ref, ret_ref, mla_ref, w_ref, o_ref):
    hr = ret_ref.shape[1]
    mix = jnp.dot(ret_ref[...], w_ref[0:hr, :], preferred_element_type=F32)
    mix += jnp.dot(mla_ref[...], w_ref[hr:, :], preferred_element_type=F32)
    o_ref[...] = x_ref[...] + mod_ref[0, 2:3, :] * mix


def _mixout(x, mod, ret, mla, w, rows_per_mod, tm=512):
    r, d = x.shape
    tm = min(tm, r)
    bpm = rows_per_mod // tm
    return pl.pallas_call(
        _mixout_kernel,
        grid=(r // tm,),
        in_specs=[pl.BlockSpec((tm, d), lambda i: (i, 0)),
                  pl.BlockSpec((1, 3, d), lambda i: (i // bpm, 0, 0)),
                  pl.BlockSpec((tm, ret.shape[1]), lambda i: (i, 0)),
                  pl.BlockSpec((tm, mla.shape[1]), lambda i: (i, 0)),
                  _const_spec(w.shape)],
        out_specs=pl.BlockSpec((tm, d), lambda i: (i, 0)),
        out_shape=jax.ShapeDtypeStruct((r, d), F32),
        compiler_params=_cparams(("parallel",)),
        name="mixout",
    )(x, mod, ret, mla, w)


def _rope_tables(n):
    pos = jnp.arange(n, dtype=F32)
    inv_r = RET_ROPE_BASE ** (-jnp.arange(0, RET_DK, 2, dtype=F32) / RET_DK)
    ang = pos[:, None] * inv_r[None, :]
    c, s = jnp.cos(ang), jnp.sin(ang)
    rcos = jnp.tile(jnp.concatenate([c, c], -1), (1, LANES // RET_DK))
    rsin = jnp.tile(jnp.concatenate([-s, s], -1), (1, LANES // RET_DK))
    ax = MLA_ROPE // 2
    inv_a = AXIAL_BASE ** (-jnp.arange(0, ax, 2, dtype=F32) / ax)
    pr = (jnp.arange(n) // GRID_W).astype(F32)[:, None] * inv_a[None, :]
    pc = (jnp.arange(n) % GRID_W).astype(F32)[:, None] * inv_a[None, :]
    cr, sr, cc, sc = jnp.cos(pr), jnp.sin(pr), jnp.cos(pc), jnp.sin(pc)
    pad1 = jnp.ones((n, LANES - MLA_ROPE), F32)
    pad0 = jnp.zeros((n, LANES - MLA_ROPE), F32)
    acos = jnp.concatenate([cr, cr, cc, cc, pad1], -1)
    asin = jnp.concatenate([-sr, sr, -sc, sc, pad0], -1)
    return rcos, rsin, acos, asin


def kernel(x, c, ctx, c_ctx, ada_w, ada_b, norm1_g, ffn1_w_in, ffn1_w_out, norm2_g, mix_w_in,
           ret_decay_fwd, ret_decay_bwd, mla_q_norm_g, mla_w_uq, mla_kv_norm_g, mla_w_ukv,
           mix_w_out, norm3_g, ffn2_w_in, ffn2_w_out, final_norm_g):
    b, n, d = x.shape
    nctx = ctx.shape[1]
    depth = ada_w.shape[0]
    heads = ret_decay_fwd.shape[1]
    assert depth == 1, "single-layer block"
    assert mla_w_uq.shape[2] == heads * (MLA_NOPE + MLA_ROPE)
    l = 0

    mrows = 8
    cvec = jnp.concatenate([c, c_ctx[None, :], jnp.zeros((mrows - b - 1, d), F32)], 0)
    mods = _adaln(cvec, ada_w[l], ada_b[l]).reshape(mrows, 9, d)
    m_lat, m_ctx = mods[:b], mods[b:b + 1]

    w1i, w1o = ffn1_w_in[l].astype(BF16), ffn1_w_out[l].astype(BF16)
    w2i, w2o = ffn2_w_in[l].astype(BF16), ffn2_w_out[l].astype(BF16)
    wmo = mix_w_out[l].astype(BF16)
    wmi = jnp.pad(mix_w_in[l], ((0, 0), (0, LANES - MLA_ROPE))).astype(BF16)
    wuq = mla_w_uq[l].reshape(MLA_Q_RANK, heads, MLA_NOPE + MLA_ROPE)
    wuq = jnp.pad(wuq, ((0, 0), (0, 0), (0, MLA_QK_PAD - MLA_NOPE - MLA_ROPE)))
    wuq = wuq.reshape(MLA_Q_RANK, heads * MLA_QK_PAD).astype(BF16)
    wukv = mla_w_ukv[l].astype(BF16)

    x2 = x.reshape(b * n, d)
    c2 = ctx.reshape(b * nctx, d)

    x2 = _ffn(x2, m_lat[:, 0:3], norm1_g[l], w1i, w1o, rows_per_mod=n)
    c2 = _ffn(c2, m_ctx[:, 0:3], norm1_g[l], w1i, w1o, rows_per_mod=b * nctx)

    tables = _rope_tables(n)
    rq, rk, rv, rg, q, k, v = _proj(x2, m_lat[:, 3:6], norm2_g[l], wmi, mla_q_norm_g[l], wuq,
                                    mla_kv_norm_g[l], wukv, heads, b, tables)
    crk, crv, ck, cv = _proj(c2, m_ctx[:, 3:6], norm2_g[l], wmi, mla_q_norm_g[l], wuq,
                             mla_kv_norm_g[l], wukv, heads, b)
    dec = jnp.stack([ret_decay_fwd[l], ret_decay_bwd[l]]).astype(F32)
    ret = _retention(dec, rq, rk, rv, rg, crk, crv)
    mla = _mla(q, k, v, ck, cv)
    x2 = _mixout(x2, m_lat[:, 3:6], ret.reshape(b * n, -1), mla.reshape(b * n, -1), wmo,
                 rows_per_mod=n)

    x2 = _ffn(x2, m_lat[:, 6:9], norm3_g[l], w2i, w2o, rows_per_mod=n, final_g=final_norm_g)
    return x2.reshape(b, n, d)
```

```python
import functools
import math

import jax
import jax.numpy as jnp
from jax import lax
from jax.experimental import pallas as pl
from jax.experimental.pallas import tpu as pltpu

GRID_W = 64
RET_DK = 64
RET_DV = 128
RET_ROPE_BASE = 10000.0
MLA_Q_RANK = 512
MLA_KV_RANK = 256
MLA_NOPE = 128
MLA_ROPE = 64
MLA_V = 128
AXIAL_BASE = 10000.0
RMS_EPS = 1e-6
GN_EPS = 1e-5

LANES = 128
MLA_QK_PAD = 2 * LANES
VMEM_LIMIT = 56 * 1024 * 1024

F32 = jnp.float32
BF16 = jnp.bfloat16


def _cparams(sem):
    return pltpu.CompilerParams(dimension_semantics=sem, vmem_limit_bytes=VMEM_LIMIT)


def _const_spec(shape):
    nd = len(shape)
    return pl.BlockSpec(shape, lambda *_: (0,) * nd, pipeline_mode=pl.Buffered(1))


def _silu(x):
    return x / (1.0 + jnp.exp(-x))


def _rms(x, g):
    return x * lax.rsqrt(jnp.mean(x * x, axis=-1, keepdims=True) + RMS_EPS) * g


def _ada_kernel(cv_ref, w_ref, b_ref, o_ref):
    s = _silu(cv_ref[...])
    o_ref[...] = jnp.dot(s, w_ref[...], preferred_element_type=F32,
                         precision=lax.Precision.HIGHEST) + b_ref[...]


def _adaln(cvec, ada_w, ada_b, tn_max=1024):
    m, d = cvec.shape
    n = ada_w.shape[1]
    tn = max(t for t in range(LANES, tn_max + 1, LANES) if n % t == 0)
    return pl.pallas_call(
        _ada_kernel,
        grid=(n // tn,),
        in_specs=[pl.BlockSpec((m, d), lambda j: (0, 0)),
                  pl.BlockSpec((d, tn), lambda j: (0, j)),
                  pl.BlockSpec((1, tn), lambda j: (0, j))],
        out_specs=pl.BlockSpec((m, tn), lambda j: (0, j)),
        out_shape=jax.ShapeDtypeStruct((m, n), F32),
        compiler_params=_cparams(("arbitrary",)),
        name="adaln",
    )(cvec, ada_w, ada_b.reshape(1, n))


def _ffn_kernel(x_ref, mod_ref, g_ref, wi_ref, wo_ref, *rest, final):
    if final:
        fg_ref, o_ref, xn_ref, acc_ref = rest
    else:
        o_ref, xn_ref, acc_ref = rest
    j = pl.program_id(1)
    nj = pl.num_programs(1)

    @pl.when(j == 0)
    def _():
        y = _rms(x_ref[...], g_ref[...])
        xn_ref[...] = (y * (1.0 + mod_ref[0, 1:2, :]) + mod_ref[0, 0:1, :]).astype(BF16)
        acc_ref[...] = jnp.zeros_like(acc_ref)

    tf = wo_ref.shape[0]
    h = jnp.dot(xn_ref[...], wi_ref[...], preferred_element_type=F32)
    act = (_silu(h[:, :tf]) * h[:, tf:]).astype(BF16)
    acc_ref[...] += jnp.dot(act, wo_ref[...], preferred_element_type=F32)

    @pl.when(j == nj - 1)
    def _():
        out = x_ref[...] + (0.5 * mod_ref[0, 2:3, :]) * acc_ref[...]
        if final:
            out = _rms(out, fg_ref[...])
        o_ref[...] = out


FFN_TF = 512


def _ffn_weights(w_in, w_out, tf=FFN_TF):
    d, f2 = w_in.shape
    nf = f2 // 2 // tf
    w = w_in.reshape(d, 2, nf, tf).transpose(0, 2, 1, 3).reshape(d, f2)
    return w.astype(BF16), w_out.astype(BF16)


def _ffn(x, mod, g, w_in, w_out, rows_per_mod, final_g=None, tm=512, tf=FFN_TF):
    r, d = x.shape
    f = w_out.shape[0]
    tm = min(tm, r)
    nf = f // tf
    bpm = rows_per_mod // tm
    final = final_g is not None
    in_specs = [
        pl.BlockSpec((tm, d), lambda i, j: (i, 0)),
        pl.BlockSpec((1, 3, d), lambda i, j: (i // bpm, 0, 0)),
        pl.BlockSpec((1, d), lambda i, j: (0, 0)),
        pl.BlockSpec((d, 2 * tf), lambda i, j: (0, j)),
        pl.BlockSpec((tf, d), lambda i, j: (j, 0)),
    ]
    args = [x, mod, g.reshape(1, d), w_in, w_out]
    if final:
        in_specs.append(pl.BlockSpec((1, d), lambda i, j: (0, 0)))
        args.append(final_g.reshape(1, d))
    return pl.pallas_call(
        functools.partial(_ffn_kernel, final=final),
        grid=(r // tm, nf),
        in_specs=in_specs,
        out_specs=pl.BlockSpec((tm, d), lambda i, j: (i, 0)),
        out_shape=jax.ShapeDtypeStruct((r, d), F32),
        scratch_shapes=[pltpu.VMEM((tm, d), BF16), pltpu.VMEM((tm, d), F32)],
        compiler_params=_cparams(("parallel", "arbitrary")),
        name="ffn_final" if final else "ffn",
    )(*args)


def _swap_halves(x, half):
    lane = lax.broadcasted_iota(jnp.int32, x.shape, 1)
    first = (lane % (2 * half)) < half
    return jnp.where(first, pltpu.roll(x, LANES - half, 1), pltpu.roll(x, half, 1))


def _rope(x, cos, sin, half):
    return x * cos + _swap_halves(x, half) * sin


def _proj_kernel(x_ref, mod_ref, g_ref, w_ref, qg_ref, wuq_ref, kvg_ref, wukv_ref, *rest,
                 heads, latent):
    if latent:
        (rcos_ref, rsin_ref, acos_ref, asin_ref,
         rq_ref, rk_ref, rv_ref, rg_ref, q_ref, k_ref, v_ref) = rest
    else:
        rk_ref, rv_ref, k_ref, v_ref = rest
    hk = heads * RET_DK
    hv = heads * RET_DV
    o_rk, o_rv, o_rg = hk, 2 * hk, 2 * hk + hv
    o_cq = o_rg + hv
    o_ckv = o_cq + MLA_Q_RANK
    o_kr = o_ckv + MLA_KV_RANK

    y = _rms(x_ref[...], g_ref[...])
    h = (y * (1.0 + mod_ref[0, 1:2, :]) + mod_ref[0, 0:1, :]).astype(BF16)
    p = jnp.dot(h, w_ref[...], preferred_element_type=F32)

    for gidx in range(hk // LANES):
        sl = slice(gidx * LANES, (gidx + 1) * LANES)
        kk = p[:, o_rk + gidx * LANES:o_rk + (gidx + 1) * LANES] * (RET_DK ** -0.5)
        if latent:
            kk = _rope(kk, rcos_ref[...], rsin_ref[...], RET_DK // 2)
            qq = _rope(p[:, sl], rcos_ref[...], rsin_ref[...], RET_DK // 2)
        for t in range(LANES // RET_DK):
            hd = gidx * (LANES // RET_DK) + t
            rk_ref[0, hd] = kk[:, t * RET_DK:(t + 1) * RET_DK].astype(BF16)
            if latent:
                rq_ref[0, hd] = qq[:, t * RET_DK:(t + 1) * RET_DK].astype(BF16)
    for hd in range(heads):
        rv_ref[0, hd] = p[:, o_rv + hd * RET_DV:o_rv + (hd + 1) * RET_DV].astype(BF16)
    if latent:
        rg_ref[0] = p[:, o_rg:o_rg + hv].astype(BF16)

    kvn = _rms(p[:, o_ckv:o_ckv + MLA_KV_RANK], kvg_ref[...]).astype(BF16)
    kv = jnp.dot(kvn, wukv_ref[...], preferred_element_type=F32)
    kr = p[:, o_kr:o_kr + LANES]
    if latent:
        kr = _rope(kr, acos_ref[...], asin_ref[...], MLA_ROPE // 4)
    kr = kr.astype(BF16)
    for hd in range(heads):
        base = hd * (MLA_NOPE + MLA_V)
        k_ref[0, hd, :, 0:MLA_NOPE] = kv[:, base:base + MLA_NOPE].astype(BF16)
        k_ref[0, hd, :, MLA_NOPE:MLA_QK_PAD] = kr
        v_ref[0, hd] = kv[:, base + MLA_NOPE:base + MLA_NOPE + MLA_V].astype(BF16)
    if latent:
        qn = _rms(p[:, o_cq:o_cq + MLA_Q_RANK], qg_ref[...]).astype(BF16)
        q = jnp.dot(qn, wuq_ref[...], preferred_element_type=F32)
        scale = (MLA_NOPE + MLA_ROPE) ** -0.5 * math.log2(math.e)
        for hd in range(heads):
            base = hd * MLA_QK_PAD
            q_ref[0, hd, :, 0:MLA_NOPE] = (q[:, base:base + MLA_NOPE] * scale).astype(BF16)
            qr = _rope(q[:, base + MLA_NOPE:base + MLA_QK_PAD],
                       acos_ref[...], asin_ref[...], MLA_ROPE // 4)
            q_ref[0, hd, :, MLA_NOPE:MLA_QK_PAD] = (qr * scale).astype(BF16)


def _proj(x, mod, g, w_in_p, qg, wuq_p, kvg, wukv, heads, batch, tables=None, tm=256):
    r, d = x.shape
    n = r // batch
    tm = min(tm, n)
    nb = n // tm
    latent = tables is not None
    row = lambda i: (i, 0)
    cst2 = lambda i: (0, 0)
    in_specs = [
        pl.BlockSpec((tm, d), row),
        pl.BlockSpec((1, 3, d), lambda i: (i // nb if latent else 0, 0, 0)),
        pl.BlockSpec((1, d), cst2),
        _const_spec(w_in_p.shape),
        pl.BlockSpec((1, MLA_Q_RANK), cst2),
        _const_spec(wuq_p.shape),
        pl.BlockSpec((1, MLA_KV_RANK), cst2),
        _const_spec(wukv.shape),
    ]
    args = [x, mod, g.reshape(1, d), w_in_p, qg.reshape(1, -1), wuq_p, kvg.reshape(1, -1), wukv]
    hmaj = lambda w: pl.BlockSpec((1, heads, tm, w), lambda i: (i // nb, 0, i % nb, 0))
    hshape = lambda w: jax.ShapeDtypeStruct((batch, heads, n, w), BF16)
    if latent:
        in_specs += [pl.BlockSpec((tm, LANES), lambda i: (i % nb, 0))] * 4
        args += list(tables)
        out_specs = [hmaj(RET_DK), hmaj(RET_DK), hmaj(RET_DV),
                     pl.BlockSpec((1, tm, heads * RET_DV), lambda i: (i // nb, i % nb, 0)),
                     hmaj(MLA_QK_PAD), hmaj(MLA_QK_PAD), hmaj(MLA_V)]
        out_shape = [hshape(RET_DK), hshape(RET_DK), hshape(RET_DV),
                     jax.ShapeDtypeStruct((batch, n, heads * RET_DV), BF16),
                     hshape(MLA_QK_PAD), hshape(MLA_QK_PAD), hshape(MLA_V)]
    else:
        out_specs = [hmaj(RET_DK), hmaj(RET_DV), hmaj(MLA_QK_PAD), hmaj(MLA_V)]
        out_shape = [hshape(RET_DK), hshape(RET_DV), hshape(MLA_QK_PAD), hshape(MLA_V)]
    return pl.pallas_call(
        functools.partial(_proj_kernel, heads=heads, latent=latent),
        grid=(r // tm,),
        in_specs=in_specs,
        out_specs=out_specs,
        out_shape=out_shape,
        compiler_params=_cparams(("parallel",)),
        name="proj_lat" if latent else "proj_ctx",
    )(*args)


def _log_sigmoid(x):
    return jnp.minimum(x, 0.0) - jnp.log1p(jnp.exp(-jnp.abs(x)))


def _ret_kernel(dec_ref, q_ref, k_ref, v_ref, g_ref, ck_ref, cv_ref, o_ref, kv_ref, st_ref,
                *, chunk, unroll):
    hd = pl.program_id(1)
    n = q_ref.shape[2]
    nctx = ck_ref.shape[2]
    nc = n // chunk
    dk = RET_DK
    lgf = _log_sigmoid(jnp.full((1, 1), dec_ref[0, hd], F32))
    lgb = _log_sigmoid(jnp.full((1, 1), dec_ref[1, hd], F32))

    def col(m):
        return lax.broadcasted_iota(jnp.int32, (m, 1), 0).astype(F32)

    def weighted_kv(k, v, zf, zb):
        kf = k.astype(F32)
        kz = jnp.concatenate([kf * zf, kf * zb], axis=-1).astype(BF16)
        return lax.dot_general(kz, v, (((0,), (0,)), ((), ())), preferred_element_type=F32)

    ic = col(nctx)
    s0 = weighted_kv(ck_ref[0, 0], cv_ref[0, 0],
                     jnp.exp(lgf * (nctx - 1.0 - ic)), jnp.exp(lgb * ic))

    i = col(chunk)
    zeta_f = jnp.exp(lgf * (chunk - 1.0 - i))
    zeta_b = jnp.exp(lgb * i)
    xi_f = jnp.exp(lgf * (i + 1.0))
    xi_b = jnp.exp(lgb * (chunk - i))
    dec_f = jnp.exp(lgf * float(chunk))
    dec_b = jnp.exp(lgb * float(chunk))
    diff = i - lax.broadcasted_iota(jnp.int32, (1, chunk), 1).astype(F32)
    dmat = jnp.where(diff >= 0.0, jnp.exp(lgf * jnp.maximum(diff, 0.0)),
                     jnp.exp(lgb * jnp.maximum(-diff, 0.0)))

    def rows(c):
        return pl.ds(pl.multiple_of(c * chunk, chunk), chunk)

    def p1(c, carry):
        kv_ref[c] = weighted_kv(k_ref[0, 0, rows(c), :], v_ref[0, 0, rows(c), :], zeta_f, zeta_b)
        return carry
    lax.fori_loop(0, nc, p1, 0, unroll=unroll)

    def p2f(c, s):
        st_ref[c, 0:dk, :] = s.astype(BF16)
        return dec_f * s + kv_ref[c, 0:dk, :]
    lax.fori_loop(0, nc, p2f, s0[0:dk])

    def p2b(t, s):
        c = nc - 1 - t
        st_ref[c, dk:2 * dk, :] = s.astype(BF16)
        return dec_b * s + kv_ref[c, dk:2 * dk, :]
    lax.fori_loop(0, nc, p2b, s0[dk:2 * dk])

    def p3(c, carry):
        q = q_ref[0, 0, rows(c), :]
        k = k_ref[0, 0, rows(c), :]
        v = v_ref[0, 0, rows(c), :]
        a = lax.dot_general(q, k, (((1,), (1,)), ((), ())), preferred_element_type=F32)
        y = jnp.dot((a * dmat).astype(BF16), v, preferred_element_type=F32)
        y += xi_f * jnp.dot(q, st_ref[c, 0:dk, :], preferred_element_type=F32)
        y += xi_b * jnp.dot(q, st_ref[c, dk:2 * dk, :], preferred_element_type=F32)
        mu = jnp.mean(y, axis=-1, keepdims=True)
        yc = y - mu
        var = jnp.mean(yc * yc, axis=-1, keepdims=True)
        yn = yc * lax.rsqrt(var + GN_EPS)
        o_ref[0, rows(c), :] = (_silu(g_ref[0, rows(c), :].astype(F32)) * yn).astype(BF16)
        return carry
    lax.fori_loop(0, nc, p3, 0, unroll=unroll)


def _retention(dec, rq, rk, rv, rg, crk, crv, chunk=256, unroll=4):
    b, h, n, _ = rq.shape
    nctx = crk.shape[2]
    chunk = min(chunk, n)
    unroll = math.gcd(unroll, n // chunk)
    hm = lambda w, rows: pl.BlockSpec((1, 1, rows, w), lambda bi, hi: (bi, hi, 0, 0))
    return pl.pallas_call(
        functools.partial(_ret_kernel, chunk=chunk, unroll=unroll),
        grid=(b, h),
        in_specs=[pl.BlockSpec(memory_space=pltpu.SMEM),
                  hm(RET_DK, n), hm(RET_DK, n), hm(RET_DV, n),
                  pl.BlockSpec((1, n, RET_DV), lambda bi, hi: (bi, 0, hi)),
                  hm(RET_DK, nctx), hm(RET_DV, nctx)],
        out_specs=pl.BlockSpec((1, n, RET_DV), lambda bi, hi: (bi, 0, hi)),
        out_shape=jax.ShapeDtypeStruct((b, n, h * RET_DV), BF16),
        scratch_shapes=[pltpu.VMEM((n // chunk, 2 * RET_DK, RET_DV), F32),
                        pltpu.VMEM((n // chunk, 2 * RET_DK, RET_DV), BF16)],
        compiler_params=_cparams(("parallel", "parallel")),
        name="retention",
    )(dec, rq, rk, rv, rg, crk, crv)


def _mla_kernel(q_ref, k_ref, vp_ref, v_ref, ck_ref, cv_ref, o_ref,
                m_ref, acc_ref, al_ref, p_ref, *, sub):
    j = pl.program_id(3)
    nj = pl.num_programs(3)
    nt = (((1,), (1,)), ((), ()))
    n_sub = q_ref.shape[2] // sub
    row_tiles = [slice(r * sub, (r + 1) * sub) for r in range(n_sub)]
    last = row_tiles[-1]

    def with_ones(v):
        return jnp.concatenate([v, jnp.ones_like(v)], axis=-1)

    def accumulate(rows, alpha, p, v1):
        acc_ref[rows, :] = jnp.tile(alpha, (1, 2)) * acc_ref[rows, :] + jnp.dot(
            p, v1, preferred_element_type=F32)

    @pl.when(j == 0)
    def _():
        s = lax.dot_general(q_ref[0, 0], ck_ref[0, 0], nt, preferred_element_type=F32)
        m = jnp.max(s, axis=-1, keepdims=True)
        p = jnp.exp2(s - m)
        m_ref[...] = jnp.broadcast_to(m, m_ref.shape)
        acc_ref[...] = jnp.dot(p.astype(BF16), with_ones(cv_ref[0, 0]),
                               preferred_element_type=F32)
        al_ref[...] = jnp.ones_like(al_ref)
        p_ref[...] = jnp.zeros_like(p_ref)

    k = k_ref[0, 0]
    v_cur = with_ones(v_ref[0, 0])
    accumulate(last, al_ref[...], p_ref[...], with_ones(vp_ref[0, 0]))

    scores, alphas, probs = {}, {}, {}

    def stage_scores(r):
        scores[r] = lax.dot_general(q_ref[0, 0, row_tiles[r], :], k, nt,
                                    preferred_element_type=F32)

    def stage_softmax(r):
        s = scores.pop(r)
        m_old = m_ref[row_tiles[r], :]
        m_new = jnp.maximum(m_old, jnp.max(s, axis=-1, keepdims=True))
        alphas[r] = jnp.exp2(m_old - m_new)
        probs[r] = jnp.exp2(s - jnp.tile(m_new, (1, k.shape[0] // LANES))).astype(BF16)
        m_ref[row_tiles[r], :] = m_new

    def stage_values(r):
        accumulate(row_tiles[r], alphas.pop(r), probs.pop(r), v_cur)

    for t in range(n_sub + 2):
        if t < n_sub:
            stage_scores(t)
        if 0 <= t - 1 < n_sub:
            stage_softmax(t - 1)
        if 0 <= t - 2 < n_sub - 1:
            stage_values(t - 2)
    al_ref[...] = alphas.pop(n_sub - 1)
    p_ref[...] = probs.pop(n_sub - 1)

    @pl.when(j == nj - 1)
    def _():
        accumulate(last, al_ref[...], p_ref[...], v_cur)
        acc = acc_ref[...]
        o_ref[0] = (acc[:, :MLA_V] / acc[:, MLA_V:]).astype(BF16)


def _mla(q, k, v, ck, cv, tq=4096, tk=1024, sub=512):
    b, h, n, w = q.shape
    nctx = ck.shape[2]
    tq = min(tq, n)
    tk = min(tk, n)
    sub = min(sub, tq)
    return pl.pallas_call(
        functools.partial(_mla_kernel, sub=sub),
        grid=(b, h, n // tq, n // tk),
        in_specs=[pl.BlockSpec((1, 1, tq, w), lambda bi, hi, i, j: (bi, hi, i, 0)),
                  pl.BlockSpec((1, 1, tk, w), lambda bi, hi, i, j: (bi, hi, j, 0)),
                  pl.BlockSpec((1, 1, tk, MLA_V),
                               lambda bi, hi, i, j: (bi, hi, jnp.maximum(j - 1, 0), 0)),
                  pl.BlockSpec((1, 1, tk, MLA_V), lambda bi, hi, i, j: (bi, hi, j, 0)),
                  pl.BlockSpec((1, 1, nctx, w), lambda bi, hi, i, j: (bi, hi, 0, 0)),
                  pl.BlockSpec((1, 1, nctx, MLA_V), lambda bi, hi, i, j: (bi, hi, 0, 0))],
        out_specs=pl.BlockSpec((1, tq, MLA_V), lambda bi, hi, i, j: (bi, i, hi)),
        out_shape=jax.ShapeDtypeStruct((b, n, h * MLA_V), BF16),
        scratch_shapes=[pltpu.VMEM((tq, LANES), F32), pltpu.VMEM((tq, 2 * MLA_V), F32),
                        pltpu.VMEM((sub, LANES), F32), pltpu.VMEM((sub, tk), BF16)],
        compiler_params=_cparams(("parallel", "parallel", "parallel", "arbitrary")),
        name="mla_attn",
    )(q, k, v, v, ck, cv)


def _mixout_kernel(x_ref, mod_ref, ret_ref, mla_ref, w_ref, o_ref):
    hr = ret_ref.shape[1]
    mix = jnp.dot(ret_ref[...], w_ref[0:hr, :], preferred_element_type=F32)
    mix += jnp.dot(mla_ref[...], w_ref[hr:, :], preferred_element_type=F32)
    o_ref[...] = x_ref[...] + mod_ref[0, 2:3, :] * mix


def _mixout(x, mod, ret, mla, w, rows_per_mod, tm=512):
    r, d = x.shape
    tm = min(tm, r)
    bpm = rows_per_mod // tm
    return pl.pallas_call(
        _mixout_kernel,
        grid=(r // tm,),
        in_specs=[pl.BlockSpec((tm, d), lambda i: (i, 0)),
                  pl.BlockSpec((1, 3, d), lambda i: (i // bpm, 0, 0)),
                  pl.BlockSpec((tm, ret.shape[1]), lambda i: (i, 0)),
                  pl.BlockSpec((tm, mla.shape[1]), lambda i: (i, 0)),
                  _const_spec(w.shape)],
        out_specs=pl.BlockSpec((tm, d), lambda i: (i, 0)),
        out_shape=jax.ShapeDtypeStruct((r, d), F32),
        compiler_params=_cparams(("parallel",)),
        name="mixout",
    )(x, mod, ret, mla, w)


def _rope_tables(n):
    pos = jnp.arange(n, dtype=F32)
    inv_r = RET_ROPE_BASE ** (-jnp.arange(0, RET_DK, 2, dtype=F32) / RET_DK)
    ang = pos[:, None] * inv_r[None, :]
    c, s = jnp.cos(ang), jnp.sin(ang)
    rcos = jnp.tile(jnp.concatenate([c, c], -1), (1, LANES // RET_DK))
    rsin = jnp.tile(jnp.concatenate([-s, s], -1), (1, LANES // RET_DK))
    ax = MLA_ROPE // 2
    inv_a = AXIAL_BASE ** (-jnp.arange(0, ax, 2, dtype=F32) / ax)
    pr = (jnp.arange(n) // GRID_W).astype(F32)[:, None] * inv_a[None, :]
    pc = (jnp.arange(n) % GRID_W).astype(F32)[:, None] * inv_a[None, :]
    cr, sr, cc, sc = jnp.cos(pr), jnp.sin(pr), jnp.cos(pc), jnp.sin(pc)
    pad1 = jnp.ones((n, LANES - MLA_ROPE), F32)
    pad0 = jnp.zeros((n, LANES - MLA_ROPE), F32)
    acos = jnp.concatenate([cr, cr, cc, cc, pad1], -1)
    asin = jnp.concatenate([-sr, sr, -sc, sc, pad0], -1)
    return rcos, rsin, acos, asin


def kernel(x, c, ctx, c_ctx, ada_w, ada_b, norm1_g, ffn1_w_in, ffn1_w_out, norm2_g, mix_w_in,
           ret_decay_fwd, ret_decay_bwd, mla_q_norm_g, mla_w_uq, mla_kv_norm_g, mla_w_ukv,
           mix_w_out, norm3_g, ffn2_w_in, ffn2_w_out, final_norm_g):
    b, n, d = x.shape
    nctx = ctx.shape[1]
    depth = ada_w.shape[0]
    heads = ret_decay_fwd.shape[1]
    assert depth == 1, "single-layer block"
    assert mla_w_uq.shape[2] == heads * (MLA_NOPE + MLA_ROPE)
    l = 0

    mrows = 8
    cvec = jnp.concatenate([c, c_ctx[None, :], jnp.zeros((mrows - b - 1, d), F32)], 0)
    mods = _adaln(cvec, ada_w[l], ada_b[l]).reshape(mrows, 9, d)
    m_lat, m_ctx = mods[:b], mods[b:b + 1]

    w1i, w1o = _ffn_weights(ffn1_w_in[l], ffn1_w_out[l])
    w2i, w2o = _ffn_weights(ffn2_w_in[l], ffn2_w_out[l])
    wmo = mix_w_out[l].astype(BF16)
    wmi = jnp.pad(mix_w_in[l], ((0, 0), (0, LANES - MLA_ROPE))).astype(BF16)
    wuq = mla_w_uq[l].reshape(MLA_Q_RANK, heads, MLA_NOPE + MLA_ROPE)
    wuq = jnp.pad(wuq, ((0, 0), (0, 0), (0, MLA_QK_PAD - MLA_NOPE - MLA_ROPE)))
    wuq = wuq.reshape(MLA_Q_RANK, heads * MLA_QK_PAD).astype(BF16)
    wukv = mla_w_ukv[l].astype(BF16)

    x2 = x.reshape(b * n, d)
    c2 = ctx.reshape(b * nctx, d)

    x2 = _ffn(x2, m_lat[:, 0:3], norm1_g[l], w1i, w1o, rows_per_mod=n)
    c2 = _ffn(c2, m_ctx[:, 0:3], norm1_g[l], w1i, w1o, rows_per_mod=b * nctx)

    tables = _rope_tables(n)
    rq, rk, rv, rg, q, k, v = _proj(x2, m_lat[:, 3:6], norm2_g[l], wmi, mla_q_norm_g[l], wuq,
                                    mla_kv_norm_g[l], wukv, heads, b, tables)
    crk, crv, ck, cv = _proj(c2, m_ctx[:, 3:6], norm2_g[l], wmi, mla_q_norm_g[l], wuq,
                             mla_kv_norm_g[l], wukv, heads, b)
    dec = jnp.stack([ret_decay_fwd[l], ret_decay_bwd[l]]).astype(F32)
    ret = _retention(dec, rq, rk, rv, rg, crk, crv)
    mla = _mla(q, k, v, ck, cv)
    x2 = _mixout(x2, m_lat[:, 3:6], ret.reshape(b * n, -1), mla.reshape(b * n, -1), wmo,
                 rows_per_mod=n)

    x2 = _ffn(x2, m_lat[:, 6:9], norm3_g[l], w2i, w2o, rows_per_mod=n, final_g=final_norm_g)
    return x2.reshape(b, n, d)
```

```python
import functools
import math

import numpy as np

import jax
import jax.numpy as jnp
from jax import lax
from jax.experimental import pallas as pl
from jax.experimental.pallas import tpu as pltpu

GRID_W = 64
RET_DK = 64
RET_DV = 128
RET_ROPE_BASE = 10000.0
MLA_Q_RANK = 512
MLA_KV_RANK = 256
MLA_NOPE = 128
MLA_ROPE = 64
MLA_V = 128
AXIAL_BASE = 10000.0
RMS_EPS = 1e-6
GN_EPS = 1e-5

LANES = 128
MLA_QK_PAD = 2 * LANES
VMEM_LIMIT = 56 * 1024 * 1024

F32 = jnp.float32
BF16 = jnp.bfloat16


def _cparams(sem):
    return pltpu.CompilerParams(dimension_semantics=sem, vmem_limit_bytes=VMEM_LIMIT)


def _const_spec(shape):
    nd = len(shape)
    return pl.BlockSpec(shape, lambda *_: (0,) * nd, pipeline_mode=pl.Buffered(1))


def _silu(x):
    return x / (1.0 + jnp.exp(-x))


def _rms(x, g):
    return x * lax.rsqrt(jnp.mean(x * x, axis=-1, keepdims=True) + RMS_EPS) * g


def _ada_kernel(ct_ref, w_ref, b_ref, o_ref, sb_ref):
    m = sb_ref.shape[0]

    @pl.when(pl.program_id(0) == 0)
    def _():
        s = _silu(ct_ref[...])
        for r in range(m):
            sb_ref[r] = jnp.broadcast_to(s[:, r:r + 1], sb_ref.shape[1:])

    sub = 8
    col_tiles = [slice(t * LANES, (t + 1) * LANES) for t in range(w_ref.shape[1] // LANES)]

    def body(kc, accs):
        rows = pl.ds(pl.multiple_of(kc * sub, sub), sub)
        s_rows = [sb_ref[r, rows, :] for r in range(m)]
        out = []
        for t, cols in enumerate(col_tiles):
            wv = w_ref[rows, cols]
            out += [accs[t * m + r] + s_rows[r] * wv for r in range(m)]
        return tuple(out)

    zero = jnp.zeros((sub, LANES), F32)
    accs = lax.fori_loop(0, w_ref.shape[0] // sub, body, (zero,) * (m * len(col_tiles)),
                         unroll=2)
    for t, cols in enumerate(col_tiles):
        for r in range(m):
            o_ref[r:r + 1, cols] = jnp.sum(accs[t * m + r], axis=0, keepdims=True) + b_ref[:, cols]


def _adaln(cvec, ada_w, ada_b, tn_max=1024):
    m, d = cvec.shape
    n = ada_w.shape[1]
    tn = max(t for t in range(LANES, tn_max + 1, LANES) if n % t == 0)
    return pl.pallas_call(
        _ada_kernel,
        grid=(n // tn,),
        in_specs=[pl.BlockSpec((d, m), lambda j: (0, 0)),
                  pl.BlockSpec((d, tn), lambda j: (0, j)),
                  pl.BlockSpec((1, tn), lambda j: (0, j))],
        out_specs=pl.BlockSpec((m, tn), lambda j: (0, j)),
        out_shape=jax.ShapeDtypeStruct((m, n), F32),
        scratch_shapes=[pltpu.VMEM((m, d, LANES), F32)],
        compiler_params=_cparams(("arbitrary",)),
        name="adaln",
    )(cvec.T, ada_w, ada_b.reshape(1, n))


def _ffn_kernel(x_ref, mod_ref, g_ref, wi_ref, wo_ref, *rest, final):
    if final:
        fg_ref, o_ref, xn_ref, acc_ref = rest
    else:
        o_ref, xn_ref, acc_ref = rest
    j = pl.program_id(1)
    nj = pl.num_programs(1)

    @pl.when(j == 0)
    def _():
        y = _rms(x_ref[...], g_ref[...])
        xn_ref[...] = (y * (1.0 + mod_ref[0, 1:2, :]) + mod_ref[0, 0:1, :]).astype(BF16)
        acc_ref[...] = jnp.zeros_like(acc_ref)

    tf = wo_ref.shape[0]
    h = jnp.dot(xn_ref[...], wi_ref[0], preferred_element_type=F32)
    act = (_silu(h[:, :tf]) * h[:, tf:]).astype(BF16)
    acc_ref[...] += jnp.dot(act, wo_ref[...], preferred_element_type=F32)

    @pl.when(j == nj - 1)
    def _():
        out = x_ref[...] + (0.5 * mod_ref[0, 2:3, :]) * acc_ref[...]
        if final:
            out = _rms(out, fg_ref[...])
        o_ref[...] = out


FFN_TF = 512


def _retile_kernel(g_ref, u_ref, o_ref):
    tf = g_ref.shape[1]
    o_ref[0, :, :tf] = g_ref[...].astype(BF16)
    o_ref[0, :, tf:] = u_ref[...].astype(BF16)


def _ffn_weights(w_in, w_out, tf=FFN_TF):
    d, f2 = w_in.shape
    nf = f2 // 2 // tf
    w = pl.pallas_call(
        _retile_kernel,
        grid=(nf,),
        in_specs=[pl.BlockSpec((d, tf), lambda j: (0, j)),
                  pl.BlockSpec((d, tf), lambda j: (0, j + nf))],
        out_specs=pl.BlockSpec((1, d, 2 * tf), lambda j: (j, 0, 0)),
        out_shape=jax.ShapeDtypeStruct((nf, d, 2 * tf), BF16),
        compiler_params=_cparams(("parallel",)),
        name="ffn_w_retile",
    )(w_in, w_in)
    return w, w_out.astype(BF16)


def _ffn(x, mod, g, w_in, w_out, rows_per_mod, final_g=None, tm=512, tf=FFN_TF):
    r, d = x.shape
    f = w_out.shape[0]
    tm = min(tm, r)
    nf = f // tf
    bpm = rows_per_mod // tm
    final = final_g is not None
    in_specs = [
        pl.BlockSpec((tm, d), lambda i, j: (i, 0)),
        pl.BlockSpec((1, 3, d), lambda i, j: (i // bpm, 0, 0)),
        pl.BlockSpec((1, d), lambda i, j: (0, 0)),
        pl.BlockSpec((1, d, 2 * tf), lambda i, j: (j, 0, 0)),
        pl.BlockSpec((tf, d), lambda i, j: (j, 0)),
    ]
    args = [x, mod, g.reshape(1, d), w_in, w_out]
    if final:
        in_specs.append(pl.BlockSpec((1, d), lambda i, j: (0, 0)))
        args.append(final_g.reshape(1, d))
    return pl.pallas_call(
        functools.partial(_ffn_kernel, final=final),
        grid=(r // tm, nf),
        in_specs=in_specs,
        out_specs=pl.BlockSpec((tm, d), lambda i, j: (i, 0)),
        out_shape=jax.ShapeDtypeStruct((r, d), F32),
        scratch_shapes=[pltpu.VMEM((tm, d), BF16), pltpu.VMEM((tm, d), F32)],
        compiler_params=_cparams(("parallel", "arbitrary")),
        name="ffn_final" if final else "ffn",
    )(*args)


def _swap_halves(x, half):
    lane = lax.broadcasted_iota(jnp.int32, x.shape, 1)
    first = (lane % (2 * half)) < half
    return jnp.where(first, pltpu.roll(x, LANES - half, 1), pltpu.roll(x, half, 1))


def _rope(x, cos, sin, half):
    return x * cos + _swap_halves(x, half) * sin


def _proj_kernel(x_ref, mod_ref, g_ref, w_ref, qg_ref, wuq_ref, kvg_ref, wukv_ref, *rest,
                 heads, latent):
    if latent:
        (rcos_ref, rsin_ref, acos_ref, asin_ref,
         rq_ref, rk_ref, rv_ref, rg_ref, q_ref, k_ref, v_ref) = rest
    else:
        rk_ref, rv_ref, k_ref, v_ref = rest
    hk = heads * RET_DK
    hv = heads * RET_DV
    o_rk, o_rv, o_rg = hk, 2 * hk, 2 * hk + hv
    o_cq = o_rg + hv
    o_ckv = o_cq + MLA_Q_RANK
    o_kr = o_ckv + MLA_KV_RANK

    y = _rms(x_ref[...], g_ref[...])
    h = (y * (1.0 + mod_ref[0, 1:2, :]) + mod_ref[0, 0:1, :]).astype(BF16)
    p = jnp.dot(h, w_ref[...], preferred_element_type=F32)

    for gidx in range(hk // LANES):
        sl = slice(gidx * LANES, (gidx + 1) * LANES)
        kk = p[:, o_rk + gidx * LANES:o_rk + (gidx + 1) * LANES] * (RET_DK ** -0.5)
        if latent:
            kk = _rope(kk, rcos_ref[...], rsin_ref[...], RET_DK // 2)
            qq = _rope(p[:, sl], rcos_ref[...], rsin_ref[...], RET_DK // 2)
        for t in range(LANES // RET_DK):
            hd = gidx * (LANES // RET_DK) + t
            rk_ref[0, hd] = kk[:, t * RET_DK:(t + 1) * RET_DK].astype(BF16)
            if latent:
                rq_ref[0, hd] = qq[:, t * RET_DK:(t + 1) * RET_DK].astype(BF16)
    for hd in range(heads):
        rv_ref[0, hd] = p[:, o_rv + hd * RET_DV:o_rv + (hd + 1) * RET_DV].astype(BF16)
    if latent:
        rg_ref[0] = p[:, o_rg:o_rg + hv].astype(BF16)

    kvn = _rms(p[:, o_ckv:o_ckv + MLA_KV_RANK], kvg_ref[...]).astype(BF16)
    kv = jnp.dot(kvn, wukv_ref[...], preferred_element_type=F32)
    kr = p[:, o_kr:o_kr + LANES]
    if latent:
        kr = _rope(kr, acos_ref[...], asin_ref[...], MLA_ROPE // 4)
    kr = kr.astype(BF16)
    for hd in range(heads):
        base = hd * (MLA_NOPE + MLA_V)
        k_ref[0, hd, :, 0:MLA_NOPE] = kv[:, base:base + MLA_NOPE].astype(BF16)
        k_ref[0, hd, :, MLA_NOPE:MLA_QK_PAD] = kr
        v_ref[0, hd] = kv[:, base + MLA_NOPE:base + MLA_NOPE + MLA_V].astype(BF16)
    if latent:
        qn = _rms(p[:, o_cq:o_cq + MLA_Q_RANK], qg_ref[...]).astype(BF16)
        q = jnp.dot(qn, wuq_ref[...], preferred_element_type=F32)
        scale = (MLA_NOPE + MLA_ROPE) ** -0.5 * math.log2(math.e)
        for hd in range(heads):
            base = hd * MLA_QK_PAD
            q_ref[0, hd, :, 0:MLA_NOPE] = (q[:, base:base + MLA_NOPE] * scale).astype(BF16)
            qr = _rope(q[:, base + MLA_NOPE:base + MLA_QK_PAD],
                       acos_ref[...], asin_ref[...], MLA_ROPE // 4)
            q_ref[0, hd, :, MLA_NOPE:MLA_QK_PAD] = (qr * scale).astype(BF16)


def _proj(x, mod, g, w_in_p, qg, wuq_p, kvg, wukv, heads, batch, tables=None, tm=256):
    r, d = x.shape
    n = r // batch
    tm = min(tm, n)
    nb = n // tm
    latent = tables is not None
    row = lambda i: (i, 0)
    cst2 = lambda i: (0, 0)
    in_specs = [
        pl.BlockSpec((tm, d), row),
        pl.BlockSpec((1, 3, d), lambda i: (i // nb if latent else 0, 0, 0)),
        pl.BlockSpec((1, d), cst2),
        _const_spec(w_in_p.shape),
        pl.BlockSpec((1, MLA_Q_RANK), cst2),
        _const_spec(wuq_p.shape),
        pl.BlockSpec((1, MLA_KV_RANK), cst2),
        _const_spec(wukv.shape),
    ]
    args = [x, mod, g.reshape(1, d), w_in_p, qg.reshape(1, -1), wuq_p, kvg.reshape(1, -1), wukv]
    hmaj = lambda w: pl.BlockSpec((1, heads, tm, w), lambda i: (i // nb, 0, i % nb, 0))
    hshape = lambda w: jax.ShapeDtypeStruct((batch, heads, n, w), BF16)
    if latent:
        in_specs += [pl.BlockSpec((tm, LANES), lambda i: (i % nb, 0))] * 4
        args += list(tables)
        out_specs = [hmaj(RET_DK), hmaj(RET_DK), hmaj(RET_DV),
                     pl.BlockSpec((1, tm, heads * RET_DV), lambda i: (i // nb, i % nb, 0)),
                     hmaj(MLA_QK_PAD), hmaj(MLA_QK_PAD), hmaj(MLA_V)]
        out_shape = [hshape(RET_DK), hshape(RET_DK), hshape(RET_DV),
                     jax.ShapeDtypeStruct((batch, n, heads * RET_DV), BF16),
                     hshape(MLA_QK_PAD), hshape(MLA_QK_PAD), hshape(MLA_V)]
    else:
        out_specs = [hmaj(RET_DK), hmaj(RET_DV), hmaj(MLA_QK_PAD), hmaj(MLA_V)]
        out_shape = [hshape(RET_DK), hshape(RET_DV), hshape(MLA_QK_PAD), hshape(MLA_V)]
    return pl.pallas_call(
        functools.partial(_proj_kernel, heads=heads, latent=latent),
        grid=(r // tm,),
        in_specs=in_specs,
        out_specs=out_specs,
        out_shape=out_shape,
        compiler_params=_cparams(("parallel",)),
        name="proj_lat" if latent else "proj_ctx",
    )(*args)


def _log_sigmoid(x):
    return jnp.minimum(x, 0.0) - jnp.log1p(jnp.exp(-jnp.abs(x)))


def _ret_kernel(dec_ref, q_ref, k_ref, v_ref, g_ref, ck_ref, cv_ref, o_ref, kv_ref, st_ref,
                *, chunk, unroll):
    hd = pl.program_id(1)
    n = q_ref.shape[2]
    nctx = ck_ref.shape[2]
    nc = n // chunk
    dk = RET_DK
    lgf = _log_sigmoid(jnp.full((1, 1), dec_ref[0, hd], F32))
    lgb = _log_sigmoid(jnp.full((1, 1), dec_ref[1, hd], F32))

    def col(m):
        return lax.broadcasted_iota(jnp.int32, (m, 1), 0).astype(F32)

    def weighted_kv(k, v, zf, zb):
        kf = k.astype(F32)
        kz = jnp.concatenate([kf * zf, kf * zb], axis=-1).astype(BF16)
        return lax.dot_general(kz, v, (((0,), (0,)), ((), ())), preferred_element_type=F32)

    ic = col(nctx)
    s0 = weighted_kv(ck_ref[0, 0], cv_ref[0, 0],
                     jnp.exp(lgf * (nctx - 1.0 - ic)), jnp.exp(lgb * ic))

    i = col(chunk)
    zeta_f = jnp.exp(lgf * (chunk - 1.0 - i))
    zeta_b = jnp.exp(lgb * i)
    xi_f = jnp.exp(lgf * (i + 1.0))
    xi_b = jnp.exp(lgb * (chunk - i))
    dec_f = jnp.exp(lgf * float(chunk))
    dec_b = jnp.exp(lgb * float(chunk))
    diff = i - lax.broadcasted_iota(jnp.int32, (1, chunk), 1).astype(F32)
    dmat = jnp.where(diff >= 0.0, jnp.exp(lgf * jnp.maximum(diff, 0.0)),
                     jnp.exp(lgb * jnp.maximum(-diff, 0.0)))

    def rows(c):
        return pl.ds(pl.multiple_of(c * chunk, chunk), chunk)

    def p1(c, carry):
        kv_ref[c] = weighted_kv(k_ref[0, 0, rows(c), :], v_ref[0, 0, rows(c), :], zeta_f, zeta_b)
        return carry
    lax.fori_loop(0, nc, p1, 0, unroll=unroll)

    def p2f(c, s):
        st_ref[c, 0:dk, :] = s.astype(BF16)
        return dec_f * s + kv_ref[c, 0:dk, :]
    lax.fori_loop(0, nc, p2f, s0[0:dk])

    def p2b(t, s):
        c = nc - 1 - t
        st_ref[c, dk:2 * dk, :] = s.astype(BF16)
        return dec_b * s + kv_ref[c, dk:2 * dk, :]
    lax.fori_loop(0, nc, p2b, s0[dk:2 * dk])

    def p3(c, carry):
        q = q_ref[0, 0, rows(c), :]
        k = k_ref[0, 0, rows(c), :]
        v = v_ref[0, 0, rows(c), :]
        a = lax.dot_general(q, k, (((1,), (1,)), ((), ())), preferred_element_type=F32)
        y = jnp.dot((a * dmat).astype(BF16), v, preferred_element_type=F32)
        y += xi_f * jnp.dot(q, st_ref[c, 0:dk, :], preferred_element_type=F32)
        y += xi_b * jnp.dot(q, st_ref[c, dk:2 * dk, :], preferred_element_type=F32)
        mu = jnp.mean(y, axis=-1, keepdims=True)
        yc = y - mu
        var = jnp.mean(yc * yc, axis=-1, keepdims=True)
        yn = yc * lax.rsqrt(var + GN_EPS)
        o_ref[0, rows(c), :] = (_silu(g_ref[0, rows(c), :].astype(F32)) * yn).astype(BF16)
        return carry
    lax.fori_loop(0, nc, p3, 0, unroll=unroll)


def _retention(dec, rq, rk, rv, rg, crk, crv, chunk=256, unroll=4):
    b, h, n, _ = rq.shape
    nctx = crk.shape[2]
    chunk = min(chunk, n)
    unroll = math.gcd(unroll, n // chunk)
    hm = lambda w, rows: pl.BlockSpec((1, 1, rows, w), lambda bi, hi: (bi, hi, 0, 0))
    return pl.pallas_call(
        functools.partial(_ret_kernel, chunk=chunk, unroll=unroll),
        grid=(b, h),
        in_specs=[pl.BlockSpec(memory_space=pltpu.SMEM),
                  hm(RET_DK, n), hm(RET_DK, n), hm(RET_DV, n),
                  pl.BlockSpec((1, n, RET_DV), lambda bi, hi: (bi, 0, hi)),
                  hm(RET_DK, nctx), hm(RET_DV, nctx)],
        out_specs=pl.BlockSpec((1, n, RET_DV), lambda bi, hi: (bi, 0, hi)),
        out_shape=jax.ShapeDtypeStruct((b, n, h * RET_DV), BF16),
        scratch_shapes=[pltpu.VMEM((n // chunk, 2 * RET_DK, RET_DV), F32),
                        pltpu.VMEM((n // chunk, 2 * RET_DK, RET_DV), BF16)],
        compiler_params=_cparams(("parallel", "parallel")),
        name="retention",
    )(dec, rq, rk, rv, rg, crk, crv)


def _mla_kernel(q_ref, k_ref, vp_ref, v_ref, ck_ref, cv_ref, o_ref,
                m_ref, acc_ref, al_ref, p_ref, *, sub):
    j = pl.program_id(3)
    nj = pl.num_programs(3)
    nt = (((1,), (1,)), ((), ()))
    n_sub = q_ref.shape[2] // sub
    row_tiles = [slice(r * sub, (r + 1) * sub) for r in range(n_sub)]
    last = row_tiles[-1]

    def with_ones(v):
        return jnp.concatenate([v, jnp.ones_like(v)], axis=-1)

    def accumulate(rows, alpha, p, v1):
        acc_ref[rows, :] = jnp.tile(alpha, (1, 2)) * acc_ref[rows, :] + jnp.dot(
            p, v1, preferred_element_type=F32)

    @pl.when(j == 0)
    def _():
        s = lax.dot_general(q_ref[0, 0], ck_ref[0, 0], nt, preferred_element_type=F32)
        m = jnp.max(s, axis=-1, keepdims=True)
        p = jnp.exp2(s - m)
        m_ref[...] = jnp.broadcast_to(m, m_ref.shape)
        acc_ref[...] = jnp.dot(p.astype(BF16), with_ones(cv_ref[0, 0]),
                               preferred_element_type=F32)
        al_ref[...] = jnp.ones_like(al_ref)
        p_ref[...] = jnp.zeros_like(p_ref)

    k = k_ref[0, 0]
    v_cur = with_ones(v_ref[0, 0])
    accumulate(last, al_ref[...], p_ref[...], with_ones(vp_ref[0, 0]))

    scores, alphas, probs = {}, {}, {}

    def stage_scores(r):
        scores[r] = lax.dot_general(q_ref[0, 0, row_tiles[r], :], k, nt,
                                    preferred_element_type=F32)

    def stage_softmax(r):
        s = scores.pop(r)
        m_old = m_ref[row_tiles[r], :]
        m_new = jnp.maximum(m_old, jnp.max(s, axis=-1, keepdims=True))
        alphas[r] = jnp.exp2(m_old - m_new)
        probs[r] = jnp.exp2(s - jnp.tile(m_new, (1, k.shape[0] // LANES))).astype(BF16)
        m_ref[row_tiles[r], :] = m_new

    def stage_values(r):
        accumulate(row_tiles[r], alphas.pop(r), probs.pop(r), v_cur)

    for t in range(n_sub + 2):
        if t < n_sub:
            stage_scores(t)
        if 0 <= t - 1 < n_sub:
            stage_softmax(t - 1)
        if 0 <= t - 2 < n_sub - 1:
            stage_values(t - 2)
    al_ref[...] = alphas.pop(n_sub - 1)
    p_ref[...] = probs.pop(n_sub - 1)

    @pl.when(j == nj - 1)
    def _():
        accumulate(last, al_ref[...], p_ref[...], v_cur)
        acc = acc_ref[...]
        o_ref[0] = (acc[:, :MLA_V] / acc[:, MLA_V:]).astype(BF16)


def _mla(q, k, v, ck, cv, tq=4096, tk=1024, sub=512):
    b, h, n, w = q.shape
    nctx = ck.shape[2]
    tq = min(tq, n)
    tk = min(tk, n)
    sub = min(sub, tq)
    return pl.pallas_call(
        functools.partial(_mla_kernel, sub=sub),
        grid=(b, h, n // tq, n // tk),
        in_specs=[pl.BlockSpec((1, 1, tq, w), lambda bi, hi, i, j: (bi, hi, i, 0)),
                  pl.BlockSpec((1, 1, tk, w), lambda bi, hi, i, j: (bi, hi, j, 0)),
                  pl.BlockSpec((1, 1, tk, MLA_V),
                               lambda bi, hi, i, j: (bi, hi, jnp.maximum(j - 1, 0), 0)),
                  pl.BlockSpec((1, 1, tk, MLA_V), lambda bi, hi, i, j: (bi, hi, j, 0)),
                  pl.BlockSpec((1, 1, nctx, w), lambda bi, hi, i, j: (bi, hi, 0, 0)),
                  pl.BlockSpec((1, 1, nctx, MLA_V), lambda bi, hi, i, j: (bi, hi, 0, 0))],
        out_specs=pl.BlockSpec((1, tq, MLA_V), lambda bi, hi, i, j: (bi, i, hi)),
        out_shape=jax.ShapeDtypeStruct((b, n, h * MLA_V), BF16),
        scratch_shapes=[pltpu.VMEM((tq, LANES), F32), pltpu.VMEM((tq, 2 * MLA_V), F32),
                        pltpu.VMEM((sub, LANES), F32), pltpu.VMEM((sub, tk), BF16)],
        compiler_params=_cparams(("parallel", "parallel", "parallel", "arbitrary")),
        name="mla_attn",
    )(q, k, v, v, ck, cv)


def _mixout_kernel(x_ref, mod_ref, ret_ref, mla_ref, w_ref, o_ref):
    hr = ret_ref.shape[1]
    mix = jnp.dot(ret_ref[...], w_ref[0:hr, :], preferred_element_type=F32)
    mix += jnp.dot(mla_ref[...], w_ref[hr:, :], preferred_element_type=F32)
    o_ref[...] = x_ref[...] + mod_ref[0, 2:3, :] * mix


def _mixout(x, mod, ret, mla, w, rows_per_mod, tm=512):
    r, d = x.shape
    tm = min(tm, r)
    bpm = rows_per_mod // tm
    return pl.pallas_call(
        _mixout_kernel,
        grid=(r // tm,),
        in_specs=[pl.BlockSpec((tm, d), lambda i: (i, 0)),
                  pl.BlockSpec((1, 3, d), lambda i: (i // bpm, 0, 0)),
                  pl.BlockSpec((tm, ret.shape[1]), lambda i: (i, 0)),
                  pl.BlockSpec((tm, mla.shape[1]), lambda i: (i, 0)),
                  _const_spec(w.shape)],
        out_specs=pl.BlockSpec((tm, d), lambda i: (i, 0)),
        out_shape=jax.ShapeDtypeStruct((r, d), F32),
        compiler_params=_cparams(("parallel",)),
        name="mixout",
    )(x, mod, ret, mla, w)


def _rope_tables(n):
    pos = np.arange(n, dtype=np.float64)
    inv_r = RET_ROPE_BASE ** (-np.arange(0, RET_DK, 2, dtype=np.float64) / RET_DK)
    ang = pos[:, None] * inv_r[None, :]
    c, s = np.cos(ang), np.sin(ang)
    rcos = np.tile(np.concatenate([c, c], -1), (1, LANES // RET_DK))
    rsin = np.tile(np.concatenate([-s, s], -1), (1, LANES // RET_DK))
    ax = MLA_ROPE // 2
    inv_a = AXIAL_BASE ** (-np.arange(0, ax, 2, dtype=np.float64) / ax)
    pr = (np.arange(n) // GRID_W).astype(np.float64)[:, None] * inv_a[None, :]
    pc = (np.arange(n) % GRID_W).astype(np.float64)[:, None] * inv_a[None, :]
    cr, sr, cc, sc = np.cos(pr), np.sin(pr), np.cos(pc), np.sin(pc)
    pad1 = np.ones((n, LANES - MLA_ROPE))
    pad0 = np.zeros((n, LANES - MLA_ROPE))
    acos = np.concatenate([cr, cr, cc, cc, pad1], -1)
    asin = np.concatenate([-sr, sr, -sc, sc, pad0], -1)
    return tuple(jnp.asarray(t.astype(np.float32)) for t in (rcos, rsin, acos, asin))


def kernel(x, c, ctx, c_ctx, ada_w, ada_b, norm1_g, ffn1_w_in, ffn1_w_out, norm2_g, mix_w_in,
           ret_decay_fwd, ret_decay_bwd, mla_q_norm_g, mla_w_uq, mla_kv_norm_g, mla_w_ukv,
           mix_w_out, norm3_g, ffn2_w_in, ffn2_w_out, final_norm_g):
    b, n, d = x.shape
    nctx = ctx.shape[1]
    depth = ada_w.shape[0]
    heads = ret_decay_fwd.shape[1]
    assert depth == 1, "single-layer block"
    assert mla_w_uq.shape[2] == heads * (MLA_NOPE + MLA_ROPE)
    l = 0

    cvec = jnp.concatenate([c, c_ctx[None, :]], 0)
    mods = _adaln(cvec, ada_w[l], ada_b[l]).reshape(b + 1, 9, d)
    m_lat, m_ctx = mods[:b], mods[b:b + 1]

    w1i, w1o = _ffn_weights(ffn1_w_in[l], ffn1_w_out[l])
    w2i, w2o = _ffn_weights(ffn2_w_in[l], ffn2_w_out[l])
    wmo = mix_w_out[l].astype(BF16)
    wmi = jnp.pad(mix_w_in[l], ((0, 0), (0, LANES - MLA_ROPE))).astype(BF16)
    wuq = mla_w_uq[l].reshape(MLA_Q_RANK, heads, MLA_NOPE + MLA_ROPE)
    wuq = jnp.pad(wuq, ((0, 0), (0, 0), (0, MLA_QK_PAD - MLA_NOPE - MLA_ROPE)))
    wuq = wuq.reshape(MLA_Q_RANK, heads * MLA_QK_PAD).astype(BF16)
    wukv = mla_w_ukv[l].astype(BF16)

    x2 = x.reshape(b * n, d)
    c2 = ctx.reshape(b * nctx, d)

    x2 = _ffn(x2, m_lat[:, 0:3], norm1_g[l], w1i, w1o, rows_per_mod=n)
    c2 = _ffn(c2, m_ctx[:, 0:3], norm1_g[l], w1i, w1o, rows_per_mod=b * nctx)

    tables = _rope_tables(n)
    rq, rk, rv, rg, q, k, v = _proj(x2, m_lat[:, 3:6], norm2_g[l], wmi, mla_q_norm_g[l], wuq,
                                    mla_kv_norm_g[l], wukv, heads, b, tables)
    crk, crv, ck, cv = _proj(c2, m_ctx[:, 3:6], norm2_g[l], wmi, mla_q_norm_g[l], wuq,
                             mla_kv_norm_g[l], wukv, heads, b)
    dec = jnp.stack([ret_decay_fwd[l], ret_decay_bwd[l]]).astype(F32)
    ret = _retention(dec, rq, rk, rv, rg, crk, crv)
    mla = _mla(q, k, v, ck, cv)
    x2 = _mixout(x2, m_lat[:, 3:6], ret.reshape(b * n, -1), mla.reshape(b * n, -1), wmo,
                 rows_per_mod=n)

    x2 = _ffn(x2, m_lat[:, 6:9], norm3_g[l], w2i, w2o, rows_per_mod=n, final_g=final_norm_g)
    return x2.reshape(b, n, d)
```

```python
import functools
import math

import numpy as np

import jax
import jax.numpy as jnp
from jax import lax
from jax.experimental import pallas as pl
from jax.experimental.pallas import tpu as pltpu

GRID_W = 64
RET_DK = 64
RET_DV = 128
RET_ROPE_BASE = 10000.0
MLA_Q_RANK = 512
MLA_KV_RANK = 256
MLA_NOPE = 128
MLA_ROPE = 64
MLA_V = 128
AXIAL_BASE = 10000.0
RMS_EPS = 1e-6
GN_EPS = 1e-5

LANES = 128
MLA_QK_PAD = 2 * LANES
VMEM_LIMIT = 56 * 1024 * 1024

F32 = jnp.float32
BF16 = jnp.bfloat16


def _cparams(sem):
    return pltpu.CompilerParams(dimension_semantics=sem, vmem_limit_bytes=VMEM_LIMIT)


def _const_spec(shape):
    nd = len(shape)
    return pl.BlockSpec(shape, lambda *_: (0,) * nd, pipeline_mode=pl.Buffered(1))


def _silu(x):
    return x / (1.0 + jnp.exp(-x))


def _rms(x, g):
    return x * lax.rsqrt(jnp.mean(x * x, axis=-1, keepdims=True) + RMS_EPS) * g


def _ada_kernel(ct_ref, w_ref, b_ref, o_ref, sb_ref):
    m = sb_ref.shape[0]

    @pl.when(pl.program_id(0) == 0)
    def _():
        s = _silu(ct_ref[...])
        for r in range(m):
            sb_ref[r] = jnp.broadcast_to(s[:, r:r + 1], sb_ref.shape[1:])

    sub = 8
    col_tiles = [slice(t * LANES, (t + 1) * LANES) for t in range(w_ref.shape[1] // LANES)]

    def body(kc, accs):
        rows = pl.ds(pl.multiple_of(kc * sub, sub), sub)
        s_rows = [sb_ref[r, rows, :] for r in range(m)]
        out = []
        for t, cols in enumerate(col_tiles):
            wv = w_ref[rows, cols]
            out += [accs[t * m + r] + s_rows[r] * wv for r in range(m)]
        return tuple(out)

    zero = jnp.zeros((sub, LANES), F32)
    accs = lax.fori_loop(0, w_ref.shape[0] // sub, body, (zero,) * (m * len(col_tiles)),
                         unroll=2)
    for t, cols in enumerate(col_tiles):
        for r in range(m):
            o_ref[r:r + 1, cols] = jnp.sum(accs[t * m + r], axis=0, keepdims=True) + b_ref[:, cols]


def _adaln(cvec, ada_w, ada_b, tn_max=1024):
    m, d = cvec.shape
    n = ada_w.shape[1]
    tn = max(t for t in range(LANES, tn_max + 1, LANES) if n % t == 0)
    return pl.pallas_call(
        _ada_kernel,
        grid=(n // tn,),
        in_specs=[pl.BlockSpec((d, m), lambda j: (0, 0)),
                  pl.BlockSpec((d, tn), lambda j: (0, j)),
                  pl.BlockSpec((1, tn), lambda j: (0, j))],
        out_specs=pl.BlockSpec((m, tn), lambda j: (0, j)),
        out_shape=jax.ShapeDtypeStruct((m, n), F32),
        scratch_shapes=[pltpu.VMEM((m, d, LANES), F32)],
        compiler_params=_cparams(("arbitrary",)),
        name="adaln",
    )(cvec.T, ada_w, ada_b.reshape(1, n))


def _ffn_kernel(x_ref, mod_ref, xnext_ref, modnext_ref, g_ref, wi_ref, wo_ref, *rest,
                final, n_slices):
    if final:
        fg_ref, o_ref, xn_ref, acc_ref = rest
    else:
        o_ref, xn_ref, acc_ref = rest
    i = pl.program_id(0)
    j = pl.program_id(1)
    nj = pl.num_programs(1)
    slot = i % 2

    def prenorm(x, mod):
        y = _rms(x, g_ref[...])
        return (y * (1.0 + mod[0, 1:2, :]) + mod[0, 0:1, :]).astype(BF16)

    @pl.when((i == 0) & (j == 0))
    def _():
        xn_ref[0] = prenorm(x_ref[...], mod_ref)
        acc_ref[...] = jnp.zeros_like(acc_ref)

    tf = wo_ref.shape[0]
    h = jnp.dot(xn_ref[slot], wi_ref[0], preferred_element_type=F32)

    rs = x_ref.shape[0] // n_slices
    rows = pl.ds(pl.multiple_of(jnp.minimum(j, n_slices - 1) * rs, rs), rs)
    xn_ref[1 - slot, rows, :] = prenorm(xnext_ref[rows, :], modnext_ref)

    act = (_silu(h[:, :tf]) * h[:, tf:]).astype(BF16)
    contrib = jnp.dot(act, wo_ref[...], preferred_element_type=F32)
    acc_ref[...] = jnp.where(j == 0, contrib, acc_ref[...] + contrib)

    @pl.when(j == nj - 1)
    def _():
        out = x_ref[...] + (0.5 * mod_ref[0, 2:3, :]) * acc_ref[...]
        if final:
            out = _rms(out, fg_ref[...])
        o_ref[...] = out


FFN_TF = 512


def _retile_kernel(g_ref, u_ref, o_ref):
    tf = g_ref.shape[1]
    o_ref[0, :, :tf] = g_ref[...].astype(BF16)
    o_ref[0, :, tf:] = u_ref[...].astype(BF16)


def _ffn_weights(w_in, w_out, tf=FFN_TF):
    d, f2 = w_in.shape
    nf = f2 // 2 // tf
    w = pl.pallas_call(
        _retile_kernel,
        grid=(nf,),
        in_specs=[pl.BlockSpec((d, tf), lambda j: (0, j)),
                  pl.BlockSpec((d, tf), lambda j: (0, j + nf))],
        out_specs=pl.BlockSpec((1, d, 2 * tf), lambda j: (j, 0, 0)),
        out_shape=jax.ShapeDtypeStruct((nf, d, 2 * tf), BF16),
        compiler_params=_cparams(("parallel",)),
        name="ffn_w_retile",
    )(w_in, w_in)
    return w, w_out.astype(BF16)


def _ffn(x, mod, g, w_in, w_out, rows_per_mod, final_g=None, tm=512, tf=FFN_TF):
    r, d = x.shape
    f = w_out.shape[0]
    tm = min(tm, r)
    nf = f // tf
    bpm = rows_per_mod // tm
    ni = r // tm
    final = final_g is not None
    n_slices = 1 << (nf.bit_length() - 1)
    nxt = lambda i: jnp.minimum(i + 1, ni - 1)
    in_specs = [
        pl.BlockSpec((tm, d), lambda i, j: (i, 0)),
        pl.BlockSpec((1, 3, d), lambda i, j: (i // bpm, 0, 0)),
        pl.BlockSpec((tm, d), lambda i, j: (nxt(i), 0)),
        pl.BlockSpec((1, 3, d), lambda i, j: (nxt(i) // bpm, 0, 0)),
        pl.BlockSpec((1, d), lambda i, j: (0, 0)),
        pl.BlockSpec((1, d, 2 * tf), lambda i, j: (j, 0, 0)),
        pl.BlockSpec((tf, d), lambda i, j: (j, 0)),
    ]
    args = [x, mod, x, mod, g.reshape(1, d), w_in, w_out]
    if final:
        in_specs.append(pl.BlockSpec((1, d), lambda i, j: (0, 0)))
        args.append(final_g.reshape(1, d))
    return pl.pallas_call(
        functools.partial(_ffn_kernel, final=final, n_slices=n_slices),
        grid=(ni, nf),
        in_specs=in_specs,
        out_specs=pl.BlockSpec((tm, d), lambda i, j: (i, 0)),
        out_shape=jax.ShapeDtypeStruct((r, d), F32),
        scratch_shapes=[pltpu.VMEM((2, tm, d), BF16), pltpu.VMEM((tm, d), F32)],
        compiler_params=_cparams(("arbitrary", "arbitrary")),
        name="ffn_final" if final else "ffn",
    )(*args)


def _swap_halves(x, half):
    lane = lax.broadcasted_iota(jnp.int32, x.shape, 1)
    first = (lane % (2 * half)) < half
    return jnp.where(first, pltpu.roll(x, LANES - half, 1), pltpu.roll(x, half, 1))


def _rope(x, cos, sin, half):
    return x * cos + _swap_halves(x, half) * sin


def _proj_kernel(x_ref, mod_ref, g_ref, w_ref, qg_ref, wuq_ref, kvg_ref, wukv_ref, *rest,
                 heads, latent):
    if latent:
        (rcos_ref, rsin_ref, acos_ref, asin_ref,
         rq_ref, rk_ref, rv_ref, rg_ref, q_ref, k_ref, v_ref) = rest
    else:
        rk_ref, rv_ref, k_ref, v_ref = rest
    hk = heads * RET_DK
    hv = heads * RET_DV
    o_rk, o_rv, o_rg = hk, 2 * hk, 2 * hk + hv
    o_cq = o_rg + hv
    o_ckv = o_cq + MLA_Q_RANK
    o_kr = o_ckv + MLA_KV_RANK

    y = _rms(x_ref[...], g_ref[...])
    h = (y * (1.0 + mod_ref[0, 1:2, :]) + mod_ref[0, 0:1, :]).astype(BF16)
    p = jnp.dot(h, w_ref[...], preferred_element_type=F32)

    for gidx in range(hk // LANES):
        sl = slice(gidx * LANES, (gidx + 1) * LANES)
        kk = p[:, o_rk + gidx * LANES:o_rk + (gidx + 1) * LANES] * (RET_DK ** -0.5)
        if latent:
            kk = _rope(kk, rcos_ref[...], rsin_ref[...], RET_DK // 2)
            qq = _rope(p[:, sl], rcos_ref[...], rsin_ref[...], RET_DK // 2)
        for t in range(LANES // RET_DK):
            hd = gidx * (LANES // RET_DK) + t
            rk_ref[0, hd] = kk[:, t * RET_DK:(t + 1) * RET_DK].astype(BF16)
            if latent:
                rq_ref[0, hd] = qq[:, t * RET_DK:(t + 1) * RET_DK].astype(BF16)
    for hd in range(heads):
        rv_ref[0, hd] = p[:, o_rv + hd * RET_DV:o_rv + (hd + 1) * RET_DV].astype(BF16)
    if latent:
        rg_ref[0] = p[:, o_rg:o_rg + hv].astype(BF16)

    kvn = _rms(p[:, o_ckv:o_ckv + MLA_KV_RANK], kvg_ref[...]).astype(BF16)
    kv = jnp.dot(kvn, wukv_ref[...], preferred_element_type=F32)
    kr = p[:, o_kr:o_kr + LANES]
    if latent:
        kr = _rope(kr, acos_ref[...], asin_ref[...], MLA_ROPE // 4)
    kr = kr.astype(BF16)
    for hd in range(heads):
        base = hd * (MLA_NOPE + MLA_V)
        k_ref[0, hd, :, 0:MLA_NOPE] = kv[:, base:base + MLA_NOPE].astype(BF16)
        k_ref[0, hd, :, MLA_NOPE:MLA_QK_PAD] = kr
        v_ref[0, hd] = kv[:, base + MLA_NOPE:base + MLA_NOPE + MLA_V].astype(BF16)
    if latent:
        qn = _rms(p[:, o_cq:o_cq + MLA_Q_RANK], qg_ref[...]).astype(BF16)
        q = jnp.dot(qn, wuq_ref[...], preferred_element_type=F32)
        scale = (MLA_NOPE + MLA_ROPE) ** -0.5 * math.log2(math.e)
        for hd in range(heads):
            base = hd * MLA_QK_PAD
            q_ref[0, hd, :, 0:MLA_NOPE] = (q[:, base:base + MLA_NOPE] * scale).astype(BF16)
            qr = _rope(q[:, base + MLA_NOPE:base + MLA_QK_PAD],
                       acos_ref[...], asin_ref[...], MLA_ROPE // 4)
            q_ref[0, hd, :, MLA_NOPE:MLA_QK_PAD] = (qr * scale).astype(BF16)


def _proj(x, mod, g, w_in_p, qg, wuq_p, kvg, wukv, heads, batch, tables=None, tm=256):
    r, d = x.shape
    n = r // batch
    tm = min(tm, n)
    nb = n // tm
    latent = tables is not None
    row = lambda i: (i, 0)
    cst2 = lambda i: (0, 0)
    in_specs = [
        pl.BlockSpec((tm, d), row),
        pl.BlockSpec((1, 3, d), lambda i: (i // nb if latent else 0, 0, 0)),
        pl.BlockSpec((1, d), cst2),
        _const_spec(w_in_p.shape),
        pl.BlockSpec((1, MLA_Q_RANK), cst2),
        _const_spec(wuq_p.shape),
        pl.BlockSpec((1, MLA_KV_RANK), cst2),
        _const_spec(wukv.shape),
    ]
    args = [x, mod, g.reshape(1, d), w_in_p, qg.reshape(1, -1), wuq_p, kvg.reshape(1, -1), wukv]
    hmaj = lambda w: pl.BlockSpec((1, heads, tm, w), lambda i: (i // nb, 0, i % nb, 0))
    hshape = lambda w: jax.ShapeDtypeStruct((batch, heads, n, w), BF16)
    if latent:
        in_specs += [pl.BlockSpec((tm, LANES), lambda i: (i % nb, 0))] * 4
        args += list(tables)
        out_specs = [hmaj(RET_DK), hmaj(RET_DK), hmaj(RET_DV),
                     pl.BlockSpec((1, tm, heads * RET_DV), lambda i: (i // nb, i % nb, 0)),
                     hmaj(MLA_QK_PAD), hmaj(MLA_QK_PAD), hmaj(MLA_V)]
        out_shape = [hshape(RET_DK), hshape(RET_DK), hshape(RET_DV),
                     jax.ShapeDtypeStruct((batch, n, heads * RET_DV), BF16),
                     hshape(MLA_QK_PAD), hshape(MLA_QK_PAD), hshape(MLA_V)]
    else:
        out_specs = [hmaj(RET_DK), hmaj(RET_DV), hmaj(MLA_QK_PAD), hmaj(MLA_V)]
        out_shape = [hshape(RET_DK), hshape(RET_DV), hshape(MLA_QK_PAD), hshape(MLA_V)]
    return pl.pallas_call(
        functools.partial(_proj_kernel, heads=heads, latent=latent),
        grid=(r // tm,),
        in_specs=in_specs,
        out_specs=out_specs,
        out_shape=out_shape,
        compiler_params=_cparams(("parallel",)),
        name="proj_lat" if latent else "proj_ctx",
    )(*args)


def _log_sigmoid(x):
    return jnp.minimum(x, 0.0) - jnp.log1p(jnp.exp(-jnp.abs(x)))


def _ret_kernel(dec_ref, q_ref, k_ref, v_ref, g_ref, ck_ref, cv_ref, o_ref, kv_ref, st_ref,
                *, chunk, unroll):
    hd = pl.program_id(1)
    n = q_ref.shape[2]
    nctx = ck_ref.shape[2]
    nc = n // chunk
    dk = RET_DK
    lgf = _log_sigmoid(jnp.full((1, 1), dec_ref[0, hd], F32))
    lgb = _log_sigmoid(jnp.full((1, 1), dec_ref[1, hd], F32))

    def col(m):
        return lax.broadcasted_iota(jnp.int32, (m, 1), 0).astype(F32)

    def weighted_kv(k, v, zf, zb):
        kf = k.astype(F32)
        kz = jnp.concatenate([kf * zf, kf * zb], axis=-1).astype(BF16)
        return lax.dot_general(kz, v, (((0,), (0,)), ((), ())), preferred_element_type=F32)

    ic = col(nctx)
    s0 = weighted_kv(ck_ref[0, 0], cv_ref[0, 0],
                     jnp.exp(lgf * (nctx - 1.0 - ic)), jnp.exp(lgb * ic))

    i = col(chunk)
    zeta_f = jnp.exp(lgf * (chunk - 1.0 - i))
    zeta_b = jnp.exp(lgb * i)
    xi_f = jnp.exp(lgf * (i + 1.0))
    xi_b = jnp.exp(lgb * (chunk - i))
    dec_f = jnp.exp(lgf * float(chunk))
    dec_b = jnp.exp(lgb * float(chunk))
    diff = i - lax.broadcasted_iota(jnp.int32, (1, chunk), 1).astype(F32)
    dmat = jnp.where(diff >= 0.0, jnp.exp(lgf * jnp.maximum(diff, 0.0)),
                     jnp.exp(lgb * jnp.maximum(-diff, 0.0)))

    def rows(c):
        return pl.ds(pl.multiple_of(c * chunk, chunk), chunk)

    def p1(c, carry):
        kv_ref[c] = weighted_kv(k_ref[0, 0, rows(c), :], v_ref[0, 0, rows(c), :], zeta_f, zeta_b)
        return carry
    lax.fori_loop(0, nc, p1, 0, unroll=unroll)

    def p2f(c, s):
        st_ref[c, 0:dk, :] = s.astype(BF16)
        return dec_f * s + kv_ref[c, 0:dk, :]
    lax.fori_loop(0, nc, p2f, s0[0:dk])

    def p2b(t, s):
        c = nc - 1 - t
        st_ref[c, dk:2 * dk, :] = s.astype(BF16)
        return dec_b * s + kv_ref[c, dk:2 * dk, :]
    lax.fori_loop(0, nc, p2b, s0[dk:2 * dk])

    def p3(c, carry):
        q = q_ref[0, 0, rows(c), :]
        k = k_ref[0, 0, rows(c), :]
        v = v_ref[0, 0, rows(c), :]
        a = lax.dot_general(q, k, (((1,), (1,)), ((), ())), preferred_element_type=F32)
        y = jnp.dot((a * dmat).astype(BF16), v, preferred_element_type=F32)
        y += xi_f * jnp.dot(q, st_ref[c, 0:dk, :], preferred_element_type=F32)
        y += xi_b * jnp.dot(q, st_ref[c, dk:2 * dk, :], preferred_element_type=F32)
        mu = jnp.mean(y, axis=-1, keepdims=True)
        yc = y - mu
        var = jnp.mean(yc * yc, axis=-1, keepdims=True)
        yn = yc * lax.rsqrt(var + GN_EPS)
        o_ref[0, rows(c), :] = (_silu(g_ref[0, rows(c), :].astype(F32)) * yn).astype(BF16)
        return carry
    lax.fori_loop(0, nc, p3, 0, unroll=unroll)


def _retention(dec, rq, rk, rv, rg, crk, crv, chunk=256, unroll=4):
    b, h, n, _ = rq.shape
    nctx = crk.shape[2]
    chunk = min(chunk, n)
    unroll = math.gcd(unroll, n // chunk)
    hm = lambda w, rows: pl.BlockSpec((1, 1, rows, w), lambda bi, hi: (bi, hi, 0, 0))
    return pl.pallas_call(
        functools.partial(_ret_kernel, chunk=chunk, unroll=unroll),
        grid=(b, h),
        in_specs=[pl.BlockSpec(memory_space=pltpu.SMEM),
                  hm(RET_DK, n), hm(RET_DK, n), hm(RET_DV, n),
                  pl.BlockSpec((1, n, RET_DV), lambda bi, hi: (bi, 0, hi)),
                  hm(RET_DK, nctx), hm(RET_DV, nctx)],
        out_specs=pl.BlockSpec((1, n, RET_DV), lambda bi, hi: (bi, 0, hi)),
        out_shape=jax.ShapeDtypeStruct((b, n, h * RET_DV), BF16),
        scratch_shapes=[pltpu.VMEM((n // chunk, 2 * RET_DK, RET_DV), F32),
                        pltpu.VMEM((n // chunk, 2 * RET_DK, RET_DV), BF16)],
        compiler_params=_cparams(("parallel", "parallel")),
        name="retention",
    )(dec, rq, rk, rv, rg, crk, crv)


def _mla_kernel(q_ref, k_ref, vp_ref, v_ref, ck_ref, cv_ref, o_ref,
                m_ref, acc_ref, s_ref, *, sub):
    j = pl.program_id(3)
    nj = pl.num_programs(3)
    nt = (((1,), (1,)), ((), ()))
    n_sub = q_ref.shape[2] // sub
    row_tiles = [slice(r * sub, (r + 1) * sub) for r in range(n_sub)]
    last = n_sub - 1

    def with_ones(v):
        return jnp.concatenate([v, jnp.ones_like(v)], axis=-1)

    def softmax_update(rows, s):
        m_old = m_ref[rows, :]
        m_new = jnp.maximum(m_old, jnp.max(s, axis=-1, keepdims=True))
        m_ref[rows, :] = m_new
        p = jnp.exp2(s - jnp.tile(m_new, (1, s.shape[1] // LANES))).astype(BF16)
        return jnp.exp2(m_old - m_new), p

    def accumulate(rows, alpha, p, v1):
        acc_ref[rows, :] = jnp.tile(alpha, (1, 2)) * acc_ref[rows, :] + jnp.dot(
            p, v1, preferred_element_type=F32)

    @pl.when(j == 0)
    def _():
        s = lax.dot_general(q_ref[0, 0], ck_ref[0, 0], nt, preferred_element_type=F32)
        m = jnp.max(s, axis=-1, keepdims=True)
        p = jnp.exp2(s - m)
        m_ref[...] = jnp.broadcast_to(m, m_ref.shape)
        acc_ref[...] = jnp.dot(p.astype(BF16), with_ones(cv_ref[0, 0]),
                               preferred_element_type=F32)
        s_ref[...] = jnp.full_like(s_ref, -jnp.inf)

    k = k_ref[0, 0]
    v_cur = with_ones(v_ref[0, 0])
    scores, alphas, probs = {}, {}, {}

    def stage_scores(r):
        scores[r] = lax.dot_general(q_ref[0, 0, row_tiles[r], :], k, nt,
                                    preferred_element_type=F32)

    def stage_softmax(r):
        alphas[r], probs[r] = softmax_update(row_tiles[r], scores.pop(r))

    def stage_values(r):
        accumulate(row_tiles[r], alphas.pop(r), probs.pop(r), v_cur)

    stage_scores(0)
    carried = softmax_update(row_tiles[last], s_ref[...])
    for t in range(1, n_sub + 1):
        if t < n_sub:
            stage_scores(t)
        if t - 1 < last:
            stage_softmax(t - 1)
        if t == 1:
            accumulate(row_tiles[last], *carried, with_ones(vp_ref[0, 0]))
        if 0 <= t - 2:
            stage_values(t - 2)
    s_ref[...] = scores.pop(last)

    @pl.when(j == nj - 1)
    def _():
        accumulate(row_tiles[last], *softmax_update(row_tiles[last], s_ref[...]), v_cur)
        acc = acc_ref[...]
        o_ref[0] = (acc[:, :MLA_V] / acc[:, MLA_V:]).astype(BF16)


def _mla(q, k, v, ck, cv, tq=4096, tk=1024, sub=512):
    b, h, n, w = q.shape
    nctx = ck.shape[2]
    tq = min(tq, n)
    tk = min(tk, n)
    sub = min(sub, tq)
    return pl.pallas_call(
        functools.partial(_mla_kernel, sub=sub),
        grid=(b, h, n // tq, n // tk),
        in_specs=[pl.BlockSpec((1, 1, tq, w), lambda bi, hi, i, j: (bi, hi, i, 0)),
                  pl.BlockSpec((1, 1, tk, w), lambda bi, hi, i, j: (bi, hi, j, 0)),
                  pl.BlockSpec((1, 1, tk, MLA_V),
                               lambda bi, hi, i, j: (bi, hi, jnp.maximum(j - 1, 0), 0)),
                  pl.BlockSpec((1, 1, tk, MLA_V), lambda bi, hi, i, j: (bi, hi, j, 0)),
                  pl.BlockSpec((1, 1, nctx, w), lambda bi, hi, i, j: (bi, hi, 0, 0)),
                  pl.BlockSpec((1, 1, nctx, MLA_V), lambda bi, hi, i, j: (bi, hi, 0, 0))],
        out_specs=pl.BlockSpec((1, tq, MLA_V), lambda bi, hi, i, j: (bi, i, hi)),
        out_shape=jax.ShapeDtypeStruct((b, n, h * MLA_V), BF16),
        scratch_shapes=[pltpu.VMEM((tq, LANES), F32), pltpu.VMEM((tq, 2 * MLA_V), F32),
                        pltpu.VMEM((sub, tk), F32)],
        compiler_params=_cparams(("parallel", "parallel", "parallel", "arbitrary")),
        name="mla_attn",
    )(q, k, v, v, ck, cv)


def _mixout_kernel(x_ref, mod_ref, ret_ref, mla_ref, w_ref, o_ref):
    hr = ret_ref.shape[1]
    mix = jnp.dot(ret_ref[...], w_ref[0:hr, :], preferred_element_type=F32)
    mix += jnp.dot(mla_ref[...], w_ref[hr:, :], preferred_element_type=F32)
    o_ref[...] = x_ref[...] + mod_ref[0, 2:3, :] * mix


def _mixout(x, mod, ret, mla, w, rows_per_mod, tm=512):
    r, d = x.shape
    tm = min(tm, r)
    bpm = rows_per_mod // tm
    return pl.pallas_call(
        _mixout_kernel,
        grid=(r // tm,),
        in_specs=[pl.BlockSpec((tm, d), lambda i: (i, 0)),
                  pl.BlockSpec((1, 3, d), lambda i: (i // bpm, 0, 0)),
                  pl.BlockSpec((tm, ret.shape[1]), lambda i: (i, 0)),
                  pl.BlockSpec((tm, mla.shape[1]), lambda i: (i, 0)),
                  _const_spec(w.shape)],
        out_specs=pl.BlockSpec((tm, d), lambda i: (i, 0)),
        out_shape=jax.ShapeDtypeStruct((r, d), F32),
        compiler_params=_cparams(("parallel",)),
        name="mixout",
    )(x, mod, ret, mla, w)


def _rope_tables(n):
    pos = np.arange(n, dtype=np.float64)
    inv_r = RET_ROPE_BASE ** (-np.arange(0, RET_DK, 2, dtype=np.float64) / RET_DK)
    ang = pos[:, None] * inv_r[None, :]
    c, s = np.cos(ang), np.sin(ang)
    rcos = np.tile(np.concatenate([c, c], -1), (1, LANES // RET_DK))
    rsin = np.tile(np.concatenate([-s, s], -1), (1, LANES // RET_DK))
    ax = MLA_ROPE // 2
    inv_a = AXIAL_BASE ** (-np.arange(0, ax, 2, dtype=np.float64) / ax)
    pr = (np.arange(n) // GRID_W).astype(np.float64)[:, None] * inv_a[None, :]
    pc = (np.arange(n) % GRID_W).astype(np.float64)[:, None] * inv_a[None, :]
    cr, sr, cc, sc = np.cos(pr), np.sin(pr), np.cos(pc), np.sin(pc)
    pad1 = np.ones((n, LANES - MLA_ROPE))
    pad0 = np.zeros((n, LANES - MLA_ROPE))
    acos = np.concatenate([cr, cr, cc, cc, pad1], -1)
    asin = np.concatenate([-sr, sr, -sc, sc, pad0], -1)
    return tuple(jnp.asarray(t.astype(np.float32)) for t in (rcos, rsin, acos, asin))


def kernel(x, c, ctx, c_ctx, ada_w, ada_b, norm1_g, ffn1_w_in, ffn1_w_out, norm2_g, mix_w_in,
           ret_decay_fwd, ret_decay_bwd, mla_q_norm_g, mla_w_uq, mla_kv_norm_g, mla_w_ukv,
           mix_w_out, norm3_g, ffn2_w_in, ffn2_w_out, final_norm_g):
    b, n, d = x.shape
    nctx = ctx.shape[1]
    depth = ada_w.shape[0]
    heads = ret_decay_fwd.shape[1]
    assert depth == 1, "single-layer block"
    assert mla_w_uq.shape[2] == heads * (MLA_NOPE + MLA_ROPE)
    l = 0

    cvec = jnp.concatenate([c, c_ctx[None, :]], 0)
    mods = _adaln(cvec, ada_w[l], ada_b[l]).reshape(b + 1, 9, d)
    m_lat, m_ctx = mods[:b], mods[b:b + 1]

    w1i, w1o = _ffn_weights(ffn1_w_in[l], ffn1_w_out[l])
    w2i, w2o = _ffn_weights(ffn2_w_in[l], ffn2_w_out[l])
    wmo = mix_w_out[l].astype(BF16)
    wmi = jnp.pad(mix_w_in[l], ((0, 0), (0, LANES - MLA_ROPE))).astype(BF16)
    wuq = mla_w_uq[l].reshape(MLA_Q_RANK, heads, MLA_NOPE + MLA_ROPE)
    wuq = jnp.pad(wuq, ((0, 0), (0, 0), (0, MLA_QK_PAD - MLA_NOPE - MLA_ROPE)))
    wuq = wuq.reshape(MLA_Q_RANK, heads * MLA_QK_PAD).astype(BF16)
    wukv = mla_w_ukv[l].astype(BF16)

    x2 = x.reshape(b * n, d)
    c2 = ctx.reshape(b * nctx, d)

    x2 = _ffn(x2, m_lat[:, 0:3], norm1_g[l], w1i, w1o, rows_per_mod=n)
    c2 = _ffn(c2, m_ctx[:, 0:3], norm1_g[l], w1i, w1o, rows_per_mod=b * nctx)

    tables = _rope_tables(n)
    rq, rk, rv, rg, q, k, v = _proj(x2, m_lat[:, 3:6], norm2_g[l], wmi, mla_q_norm_g[l], wuq,
                                    mla_kv_norm_g[l], wukv, heads, b, tables)
    crk, crv, ck, cv = _proj(c2, m_ctx[:, 3:6], norm2_g[l], wmi, mla_q_norm_g[l], wuq,
                             mla_kv_norm_g[l], wukv, heads, b)
    dec = jnp.stack([ret_decay_fwd[l], ret_decay_bwd[l]]).astype(F32)
    ret = _retention(dec, rq, rk, rv, rg, crk, crv)
    mla = _mla(q, k, v, ck, cv)
    x2 = _mixout(x2, m_lat[:, 3:6], ret.reshape(b * n, -1), mla.reshape(b * n, -1), wmo,
                 rows_per_mod=n)

    x2 = _ffn(x2, m_lat[:, 6:9], norm3_g[l], w2i, w2o, rows_per_mod=n, final_g=final_norm_g)
    return x2.reshape(b, n, d)
```

```python
import functools
import math

import numpy as np

import jax
import jax.numpy as jnp
from jax import lax
from jax.experimental import pallas as pl
from jax.experimental.pallas import tpu as pltpu

GRID_W = 64
RET_DK = 64
RET_DV = 128
RET_ROPE_BASE = 10000.0
MLA_Q_RANK = 512
MLA_KV_RANK = 256
MLA_NOPE = 128
MLA_ROPE = 64
MLA_V = 128
AXIAL_BASE = 10000.0
RMS_EPS = 1e-6
GN_EPS = 1e-5

LANES = 128
MLA_QK_PAD = 2 * LANES
VMEM_LIMIT = 56 * 1024 * 1024

F32 = jnp.float32
BF16 = jnp.bfloat16


def _cparams(sem):
    return pltpu.CompilerParams(dimension_semantics=sem, vmem_limit_bytes=VMEM_LIMIT)


def _const_spec(shape):
    nd = len(shape)
    return pl.BlockSpec(shape, lambda *_: (0,) * nd, pipeline_mode=pl.Buffered(1))


def _silu(x):
    return x / (1.0 + jnp.exp(-x))


def _rms(x, g):
    return x * lax.rsqrt(jnp.mean(x * x, axis=-1, keepdims=True) + RMS_EPS) * g


def _ada_kernel(ct_ref, w_ref, b_ref, o_ref, sb_ref):
    m = sb_ref.shape[0]

    @pl.when(pl.program_id(0) == 0)
    def _():
        s = _silu(ct_ref[...])
        for r in range(m):
            sb_ref[r] = jnp.broadcast_to(s[:, r:r + 1], sb_ref.shape[1:])

    sub = 8
    col_tiles = [slice(t * LANES, (t + 1) * LANES) for t in range(w_ref.shape[1] // LANES)]

    def body(kc, accs):
        rows = pl.ds(pl.multiple_of(kc * sub, sub), sub)
        s_rows = [sb_ref[r, rows, :] for r in range(m)]
        out = []
        for t, cols in enumerate(col_tiles):
            wv = w_ref[rows, cols]
            out += [accs[t * m + r] + s_rows[r] * wv for r in range(m)]
        return tuple(out)

    zero = jnp.zeros((sub, LANES), F32)
    accs = lax.fori_loop(0, w_ref.shape[0] // sub, body, (zero,) * (m * len(col_tiles)),
                         unroll=2)
    for t, cols in enumerate(col_tiles):
        for r in range(m):
            o_ref[r:r + 1, cols] = jnp.sum(accs[t * m + r], axis=0, keepdims=True) + b_ref[:, cols]


def _adaln(cvec, ada_w, ada_b, tn_max=1024):
    m, d = cvec.shape
    n = ada_w.shape[1]
    tn = max(t for t in range(LANES, tn_max + 1, LANES) if n % t == 0)
    return pl.pallas_call(
        _ada_kernel,
        grid=(n // tn,),
        in_specs=[pl.BlockSpec((d, m), lambda j: (0, 0)),
                  pl.BlockSpec((d, tn), lambda j: (0, j)),
                  pl.BlockSpec((1, tn), lambda j: (0, j))],
        out_specs=pl.BlockSpec((m, tn), lambda j: (0, j)),
        out_shape=jax.ShapeDtypeStruct((m, n), F32),
        scratch_shapes=[pltpu.VMEM((m, d, LANES), F32)],
        compiler_params=_cparams(("arbitrary",)),
        name="adaln",
    )(cvec.T, ada_w, ada_b.reshape(1, n))


def _ffn_kernel(x_ref, mod_ref, g_ref, wi_ref, wo_ref, *rest, final):
    if final:
        fg_ref, o_ref, xn_ref = rest
    else:
        o_ref, xn_ref = rest
    j = pl.program_id(1)
    nj = pl.num_programs(1)

    @pl.when(j == 0)
    def _():
        y = _rms(x_ref[...], g_ref[...])
        xn_ref[...] = (y * (1.0 + mod_ref[0, 1:2, :]) + mod_ref[0, 0:1, :]).astype(BF16)
        o_ref[...] = jnp.zeros_like(o_ref)

    tf = wo_ref.shape[0]
    h = jnp.dot(xn_ref[...], wi_ref[0], preferred_element_type=F32)
    act = (_silu(h[:, :tf]) * h[:, tf:]).astype(BF16)
    o_ref[...] += jnp.dot(act, wo_ref[...], preferred_element_type=F32)

    @pl.when(j == nj - 1)
    def _():
        out = x_ref[...] + (0.5 * mod_ref[0, 2:3, :]) * o_ref[...]
        if final:
            out = _rms(out, fg_ref[...])
        o_ref[...] = out


FFN_TF = 512


def _retile_kernel(g_ref, u_ref, o_ref):
    tf = g_ref.shape[1]
    o_ref[0, :, :tf] = g_ref[...].astype(BF16)
    o_ref[0, :, tf:] = u_ref[...].astype(BF16)


def _ffn_weights(w_in, w_out, tf=FFN_TF):
    d, f2 = w_in.shape
    nf = f2 // 2 // tf
    w = pl.pallas_call(
        _retile_kernel,
        grid=(nf,),
        in_specs=[pl.BlockSpec((d, tf), lambda j: (0, j)),
                  pl.BlockSpec((d, tf), lambda j: (0, j + nf))],
        out_specs=pl.BlockSpec((1, d, 2 * tf), lambda j: (j, 0, 0)),
        out_shape=jax.ShapeDtypeStruct((nf, d, 2 * tf), BF16),
        compiler_params=_cparams(("parallel",)),
        name="ffn_w_retile",
    )(w_in, w_in)
    return w, w_out.astype(BF16)


def _ffn(x, mod, g, w_in, w_out, rows_per_mod, final_g=None, tm=512, tf=FFN_TF):
    r, d = x.shape
    f = w_out.shape[0]
    tm = min(tm, r)
    nf = f // tf
    bpm = rows_per_mod // tm
    final = final_g is not None
    in_specs = [
        pl.BlockSpec((tm, d), lambda i, j: (i, 0)),
        pl.BlockSpec((1, 3, d), lambda i, j: (i // bpm, 0, 0)),
        pl.BlockSpec((1, d), lambda i, j: (0, 0)),
        pl.BlockSpec((1, d, 2 * tf), lambda i, j: (j, 0, 0)),
        pl.BlockSpec((tf, d), lambda i, j: (j, 0)),
    ]
    args = [x, mod, g.reshape(1, d), w_in, w_out]
    if final:
        in_specs.append(pl.BlockSpec((1, d), lambda i, j: (0, 0)))
        args.append(final_g.reshape(1, d))
    return pl.pallas_call(
        functools.partial(_ffn_kernel, final=final),
        grid=(r // tm, nf),
        in_specs=in_specs,
        out_specs=pl.BlockSpec((tm, d), lambda i, j: (i, 0)),
        out_shape=jax.ShapeDtypeStruct((r, d), F32),
        scratch_shapes=[pltpu.VMEM((tm, d), BF16)],
        compiler_params=_cparams(("parallel", "arbitrary")),
        name="ffn_final" if final else "ffn",
    )(*args)


def _swap_halves(x, half):
    lane = lax.broadcasted_iota(jnp.int32, x.shape, 1)
    first = (lane % (2 * half)) < half
    return jnp.where(first, pltpu.roll(x, LANES - half, 1), pltpu.roll(x, half, 1))


def _rope(x, cos, sin, half):
    return x * cos + _swap_halves(x, half) * sin


def _proj_kernel(x_ref, mod_ref, g_ref, w_ref, qg_ref, wuq_ref, kvg_ref, wukv_ref, *rest,
                 heads, latent):
    if latent:
        (rcos_ref, rsin_ref, acos_ref, asin_ref,
         rq_ref, rk_ref, rv_ref, rg_ref, q_ref, k_ref, v_ref) = rest
    else:
        rk_ref, rv_ref, k_ref, v_ref = rest
    hk = heads * RET_DK
    hv = heads * RET_DV
    o_rk, o_rv, o_rg = hk, 2 * hk, 2 * hk + hv
    o_cq = o_rg + hv
    o_ckv = o_cq + MLA_Q_RANK
    o_kr = o_ckv + MLA_KV_RANK

    y = _rms(x_ref[...], g_ref[...])
    h = (y * (1.0 + mod_ref[0, 1:2, :]) + mod_ref[0, 0:1, :]).astype(BF16)
    p = jnp.dot(h, w_ref[...], preferred_element_type=F32)

    for gidx in range(hk // LANES):
        sl = slice(gidx * LANES, (gidx + 1) * LANES)
        kk = p[:, o_rk + gidx * LANES:o_rk + (gidx + 1) * LANES] * (RET_DK ** -0.5)
        if latent:
            kk = _rope(kk, rcos_ref[...], rsin_ref[...], RET_DK // 2)
            qq = _rope(p[:, sl], rcos_ref[...], rsin_ref[...], RET_DK // 2)
        for t in range(LANES // RET_DK):
            hd = gidx * (LANES // RET_DK) + t
            rk_ref[0, hd] = kk[:, t * RET_DK:(t + 1) * RET_DK].astype(BF16)
            if latent:
                rq_ref[0, hd] = qq[:, t * RET_DK:(t + 1) * RET_DK].astype(BF16)
    for hd in range(heads):
        rv_ref[0, hd] = p[:, o_rv + hd * RET_DV:o_rv + (hd + 1) * RET_DV].astype(BF16)
    if latent:
        rg_ref[0] = p[:, o_rg:o_rg + hv].astype(BF16)

    kvn = _rms(p[:, o_ckv:o_ckv + MLA_KV_RANK], kvg_ref[...]).astype(BF16)
    kv = jnp.dot(kvn, wukv_ref[...], preferred_element_type=F32)
    kr = p[:, o_kr:o_kr + LANES]
    if latent:
        kr = _rope(kr, acos_ref[...], asin_ref[...], MLA_ROPE // 4)
    kr = kr.astype(BF16)
    for hd in range(heads):
        base = hd * (MLA_NOPE + MLA_V)
        k_ref[0, hd, :, 0:MLA_NOPE] = kv[:, base:base + MLA_NOPE].astype(BF16)
        k_ref[0, hd, :, MLA_NOPE:MLA_QK_PAD] = kr
        v_ref[0, hd] = kv[:, base + MLA_NOPE:base + MLA_NOPE + MLA_V].astype(BF16)
    if latent:
        qn = _rms(p[:, o_cq:o_cq + MLA_Q_RANK], qg_ref[...]).astype(BF16)
        q = jnp.dot(qn, wuq_ref[...], preferred_element_type=F32)
        scale = (MLA_NOPE + MLA_ROPE) ** -0.5 * math.log2(math.e)
        for hd in range(heads):
            base = hd * MLA_QK_PAD
            q_ref[0, hd, :, 0:MLA_NOPE] = (q[:, base:base + MLA_NOPE] * scale).astype(BF16)
            qr = _rope(q[:, base + MLA_NOPE:base + MLA_QK_PAD],
                       acos_ref[...], asin_ref[...], MLA_ROPE // 4)
            q_ref[0, hd, :, MLA_NOPE:MLA_QK_PAD] = (qr * scale).astype(BF16)


def _proj(x, mod, g, w_in_p, qg, wuq_p, kvg, wukv, heads, batch, tables=None, tm=256):
    r, d = x.shape
    n = r // batch
    tm = min(tm, n)
    nb = n // tm
    latent = tables is not None
    row = lambda i: (i, 0)
    cst2 = lambda i: (0, 0)
    in_specs = [
        pl.BlockSpec((tm, d), row),
        pl.BlockSpec((1, 3, d), lambda i: (i // nb if latent else 0, 0, 0)),
        pl.BlockSpec((1, d), cst2),
        _const_spec(w_in_p.shape),
        pl.BlockSpec((1, MLA_Q_RANK), cst2),
        _const_spec(wuq_p.shape),
        pl.BlockSpec((1, MLA_KV_RANK), cst2),
        _const_spec(wukv.shape),
    ]
    args = [x, mod, g.reshape(1, d), w_in_p, qg.reshape(1, -1), wuq_p, kvg.reshape(1, -1), wukv]
    hmaj = lambda w: pl.BlockSpec((1, heads, tm, w), lambda i: (i // nb, 0, i % nb, 0))
    hshape = lambda w: jax.ShapeDtypeStruct((batch, heads, n, w), BF16)
    if latent:
        in_specs += [pl.BlockSpec((tm, LANES), lambda i: (i % nb, 0))] * 4
        args += list(tables)
        out_specs = [hmaj(RET_DK), hmaj(RET_DK), hmaj(RET_DV),
                     pl.BlockSpec((1, tm, heads * RET_DV), lambda i: (i // nb, i % nb, 0)),
                     hmaj(MLA_QK_PAD), hmaj(MLA_QK_PAD), hmaj(MLA_V)]
        out_shape = [hshape(RET_DK), hshape(RET_DK), hshape(RET_DV),
                     jax.ShapeDtypeStruct((batch, n, heads * RET_DV), BF16),
                     hshape(MLA_QK_PAD), hshape(MLA_QK_PAD), hshape(MLA_V)]
    else:
        out_specs = [hmaj(RET_DK), hmaj(RET_DV), hmaj(MLA_QK_PAD), hmaj(MLA_V)]
        out_shape = [hshape(RET_DK), hshape(RET_DV), hshape(MLA_QK_PAD), hshape(MLA_V)]
    return pl.pallas_call(
        functools.partial(_proj_kernel, heads=heads, latent=latent),
        grid=(r // tm,),
        in_specs=in_specs,
        out_specs=out_specs,
        out_shape=out_shape,
        compiler_params=_cparams(("parallel",)),
        name="proj_lat" if latent else "proj_ctx",
    )(*args)


def _log_sigmoid(x):
    return jnp.minimum(x, 0.0) - jnp.log1p(jnp.exp(-jnp.abs(x)))


def _ret_kernel(dec_ref, q_ref, k_ref, v_ref, g_ref, ck_ref, cv_ref, o_ref, kv_ref, st_ref,
                *, chunk, unroll):
    hd = pl.program_id(1)
    n = q_ref.shape[2]
    nctx = ck_ref.shape[2]
    nc = n // chunk
    dk = RET_DK
    lgf = _log_sigmoid(jnp.full((1, 1), dec_ref[0, hd], F32))
    lgb = _log_sigmoid(jnp.full((1, 1), dec_ref[1, hd], F32))

    def col(m):
        return lax.broadcasted_iota(jnp.int32, (m, 1), 0).astype(F32)

    def weighted_kv(k, v, zf, zb):
        kf = k.astype(F32)
        kz = jnp.concatenate([kf * zf, kf * zb], axis=-1).astype(BF16)
        return lax.dot_general(kz, v, (((0,), (0,)), ((), ())), preferred_element_type=F32)

    ic = col(nctx)
    s0 = weighted_kv(ck_ref[0, 0], cv_ref[0, 0],
                     jnp.exp(lgf * (nctx - 1.0 - ic)), jnp.exp(lgb * ic))

    i = col(chunk)
    zeta_f = jnp.exp(lgf * (chunk - 1.0 - i))
    zeta_b = jnp.exp(lgb * i)
    xi_f = jnp.exp(lgf * (i + 1.0))
    xi_b = jnp.exp(lgb * (chunk - i))
    dec_f = jnp.exp(lgf * float(chunk))
    dec_b = jnp.exp(lgb * float(chunk))
    diff = i - lax.broadcasted_iota(jnp.int32, (1, chunk), 1).astype(F32)
    dmat = jnp.where(diff >= 0.0, jnp.exp(lgf * jnp.maximum(diff, 0.0)),
                     jnp.exp(lgb * jnp.maximum(-diff, 0.0)))

    def rows(c):
        return pl.ds(pl.multiple_of(c * chunk, chunk), chunk)

    def p1(c, carry):
        kv_ref[c] = weighted_kv(k_ref[0, 0, rows(c), :], v_ref[0, 0, rows(c), :], zeta_f, zeta_b)
        return carry
    lax.fori_loop(0, nc, p1, 0, unroll=unroll)

    dv = v_ref.shape[3]

    def p2f(c, s):
        st_ref[c, :, 0:dv] = s.astype(BF16)
        return dec_f * s + kv_ref[c, 0:dk, :]
    lax.fori_loop(0, nc, p2f, s0[0:dk])

    def p2b(t, s):
        c = nc - 1 - t
        st_ref[c, :, dv:2 * dv] = s.astype(BF16)
        return dec_b * s + kv_ref[c, dk:2 * dk, :]
    lax.fori_loop(0, nc, p2b, s0[dk:2 * dk])

    def p3(c, carry):
        q = q_ref[0, 0, rows(c), :]
        k = k_ref[0, 0, rows(c), :]
        v = v_ref[0, 0, rows(c), :]
        a = lax.dot_general(q, k, (((1,), (1,)), ((), ())), preferred_element_type=F32)
        y = jnp.dot((a * dmat).astype(BF16), v, preferred_element_type=F32)
        cross = jnp.dot(q, st_ref[c], preferred_element_type=F32)
        y += xi_f * cross[:, 0:dv] + xi_b * cross[:, dv:2 * dv]
        mu = jnp.mean(y, axis=-1, keepdims=True)
        yc = y - mu
        var = jnp.mean(yc * yc, axis=-1, keepdims=True)
        yn = yc * lax.rsqrt(var + GN_EPS)
        o_ref[0, rows(c), :] = (_silu(g_ref[0, rows(c), :].astype(F32)) * yn).astype(BF16)
        return carry
    lax.fori_loop(0, nc, p3, 0, unroll=unroll)


def _retention(dec, rq, rk, rv, rg, crk, crv, chunk=256, unroll=8):
    b, h, n, _ = rq.shape
    nctx = crk.shape[2]
    chunk = min(chunk, n)
    unroll = math.gcd(unroll, n // chunk)
    hm = lambda w, rows: pl.BlockSpec((1, 1, rows, w), lambda bi, hi: (bi, hi, 0, 0))
    return pl.pallas_call(
        functools.partial(_ret_kernel, chunk=chunk, unroll=unroll),
        grid=(b, h),
        in_specs=[pl.BlockSpec(memory_space=pltpu.SMEM),
                  hm(RET_DK, n), hm(RET_DK, n), hm(RET_DV, n),
                  pl.BlockSpec((1, n, RET_DV), lambda bi, hi: (bi, 0, hi)),
                  hm(RET_DK, nctx), hm(RET_DV, nctx)],
        out_specs=pl.BlockSpec((1, n, RET_DV), lambda bi, hi: (bi, 0, hi)),
        out_shape=jax.ShapeDtypeStruct((b, n, h * RET_DV), BF16),
        scratch_shapes=[pltpu.VMEM((n // chunk, 2 * RET_DK, RET_DV), F32),
                        pltpu.VMEM((n // chunk, RET_DK, 2 * RET_DV), BF16)],
        compiler_params=_cparams(("parallel", "parallel")),
        name="retention",
    )(dec, rq, rk, rv, rg, crk, crv)


def _mla_kernel(q_ref, k_ref, vp_ref, v_ref, ck_ref, cv_ref, o_ref,
                m_ref, acc_ref, s_ref, *, sub):
    j = pl.program_id(3)
    nj = pl.num_programs(3)
    nt = (((1,), (1,)), ((), ()))
    n_sub = q_ref.shape[2] // sub
    row_tiles = [slice(r * sub, (r + 1) * sub) for r in range(n_sub)]
    last = n_sub - 1

    def with_ones(v):
        return jnp.concatenate([v, jnp.ones_like(v)], axis=-1)

    def softmax_update(rows, s):
        m_old = m_ref[rows, :]
        m_new = jnp.maximum(m_old, jnp.max(s, axis=-1, keepdims=True))
        m_ref[rows, :] = m_new
        p = jnp.exp2(s - jnp.tile(m_new, (1, s.shape[1] // LANES))).astype(BF16)
        return jnp.exp2(m_old - m_new), p

    def accumulate(rows, alpha, p, v1):
        acc_ref[rows, :] = jnp.tile(alpha, (1, 2)) * acc_ref[rows, :] + jnp.dot(
            p, v1, preferred_element_type=F32)

    @pl.when(j == 0)
    def _():
        s = lax.dot_general(q_ref[0, 0], ck_ref[0, 0], nt, preferred_element_type=F32)
        m = jnp.max(s, axis=-1, keepdims=True)
        p = jnp.exp2(s - m)
        m_ref[...] = jnp.broadcast_to(m, m_ref.shape)
        acc_ref[...] = jnp.dot(p.astype(BF16), with_ones(cv_ref[0, 0]),
                               preferred_element_type=F32)
        s_ref[...] = jnp.full_like(s_ref, -jnp.inf)

    k = k_ref[0, 0]
    v_cur = with_ones(v_ref[0, 0])
    scores, alphas, probs = {}, {}, {}

    def stage_scores(r):
        scores[r] = lax.dot_general(q_ref[0, 0, row_tiles[r], :], k, nt,
                                    preferred_element_type=F32)

    def stage_softmax(r):
        alphas[r], probs[r] = softmax_update(row_tiles[r], scores.pop(r))

    def stage_values(r):
        accumulate(row_tiles[r], alphas.pop(r), probs.pop(r), v_cur)

    stage_scores(0)
    carried = softmax_update(row_tiles[last], s_ref[...])
    for t in range(1, n_sub + 1):
        if t < n_sub:
            stage_scores(t)
        if t - 1 < last:
            stage_softmax(t - 1)
        if t == 1:
            accumulate(row_tiles[last], *carried, with_ones(vp_ref[0, 0]))
        if 0 <= t - 2:
            stage_values(t - 2)
    s_ref[...] = scores.pop(last)

    @pl.when(j == nj - 1)
    def _():
        accumulate(row_tiles[last], *softmax_update(row_tiles[last], s_ref[...]), v_cur)
        acc = acc_ref[...]
        o_ref[0] = (acc[:, :MLA_V] / acc[:, MLA_V:]).astype(BF16)


def _mla(q, k, v, ck, cv, tq=4096, tk=2048, sub=512):
    b, h, n, w = q.shape
    nctx = ck.shape[2]
    tq = min(tq, n)
    tk = min(tk, n)
    sub = min(sub, tq)
    return pl.pallas_call(
        functools.partial(_mla_kernel, sub=sub),
        grid=(b, h, n // tq, n // tk),
        in_specs=[pl.BlockSpec((1, 1, tq, w), lambda bi, hi, i, j: (bi, hi, i, 0)),
                  pl.BlockSpec((1, 1, tk, w), lambda bi, hi, i, j: (bi, hi, j, 0)),
                  pl.BlockSpec((1, 1, tk, MLA_V),
                               lambda bi, hi, i, j: (bi, hi, jnp.maximum(j - 1, 0), 0)),
                  pl.BlockSpec((1, 1, tk, MLA_V), lambda bi, hi, i, j: (bi, hi, j, 0)),
                  pl.BlockSpec((1, 1, nctx, w), lambda bi, hi, i, j: (bi, hi, 0, 0)),
                  pl.BlockSpec((1, 1, nctx, MLA_V), lambda bi, hi, i, j: (bi, hi, 0, 0))],
        out_specs=pl.BlockSpec((1, tq, MLA_V), lambda bi, hi, i, j: (bi, i, hi)),
        out_shape=jax.ShapeDtypeStruct((b, n, h * MLA_V), BF16),
        scratch_shapes=[pltpu.VMEM((tq, LANES), F32), pltpu.VMEM((tq, 2 * MLA_V), F32),
                        pltpu.VMEM((sub, tk), F32)],
        compiler_params=_cparams(("parallel", "parallel", "parallel", "arbitrary")),
        name="mla_attn",
    )(q, k, v, v, ck, cv)


def _mixout_kernel(x_ref, mod_ref, ret_ref, mla_ref, w_ref, o_ref):
    hr = ret_ref.shape[1]
    mix = jnp.dot(ret_ref[...], w_ref[0:hr, :], preferred_element_type=F32)
    mix += jnp.dot(mla_ref[...], w_ref[hr:, :], preferred_element_type=F32)
    o_ref[...] = x_ref[...] + mod_ref[0, 2:3, :] * mix


def _mixout(x, mod, ret, mla, w, rows_per_mod, tm=512):
    r, d = x.shape
    tm = min(tm, r)
    bpm = rows_per_mod // tm
    return pl.pallas_call(
        _mixout_kernel,
        grid=(r // tm,),
        in_specs=[pl.BlockSpec((tm, d), lambda i: (i, 0)),
                  pl.BlockSpec((1, 3, d), lambda i: (i // bpm, 0, 0)),
                  pl.BlockSpec((tm, ret.shape[1]), lambda i: (i, 0)),
                  pl.BlockSpec((tm, mla.shape[1]), lambda i: (i, 0)),
                  _const_spec(w.shape)],
        out_specs=pl.BlockSpec((tm, d), lambda i: (i, 0)),
        out_shape=jax.ShapeDtypeStruct((r, d), F32),
        compiler_params=_cparams(("parallel",)),
        name="mixout",
    )(x, mod, ret, mla, w)


def _rope_tables(n):
    pos = np.arange(n, dtype=np.float64)
    inv_r = RET_ROPE_BASE ** (-np.arange(0, RET_DK, 2, dtype=np.float64) / RET_DK)
    ang = pos[:, None] * inv_r[None, :]
    c, s = np.cos(ang), np.sin(ang)
    rcos = np.tile(np.concatenate([c, c], -1), (1, LANES // RET_DK))
    rsin = np.tile(np.concatenate([-s, s], -1), (1, LANES // RET_DK))
    ax = MLA_ROPE // 2
    inv_a = AXIAL_BASE ** (-np.arange(0, ax, 2, dtype=np.float64) / ax)
    pr = (np.arange(n) // GRID_W).astype(np.float64)[:, None] * inv_a[None, :]
    pc = (np.arange(n) % GRID_W).astype(np.float64)[:, None] * inv_a[None, :]
    cr, sr, cc, sc = np.cos(pr), np.sin(pr), np.cos(pc), np.sin(pc)
    pad1 = np.ones((n, LANES - MLA_ROPE))
    pad0 = np.zeros((n, LANES - MLA_ROPE))
    acos = np.concatenate([cr, cr, cc, cc, pad1], -1)
    asin = np.concatenate([-sr, sr, -sc, sc, pad0], -1)
    return tuple(jnp.asarray(t.astype(np.float32)) for t in (rcos, rsin, acos, asin))


def kernel(x, c, ctx, c_ctx, ada_w, ada_b, norm1_g, ffn1_w_in, ffn1_w_out, norm2_g, mix_w_in,
           ret_decay_fwd, ret_decay_bwd, mla_q_norm_g, mla_w_uq, mla_kv_norm_g, mla_w_ukv,
           mix_w_out, norm3_g, ffn2_w_in, ffn2_w_out, final_norm_g):
    b, n, d = x.shape
    nctx = ctx.shape[1]
    depth = ada_w.shape[0]
    heads = ret_decay_fwd.shape[1]
    assert depth == 1, "single-layer block"
    assert mla_w_uq.shape[2] == heads * (MLA_NOPE + MLA_ROPE)
    l = 0

    cvec = jnp.concatenate([c, c_ctx[None, :]], 0)
    mods = _adaln(cvec, ada_w[l], ada_b[l]).reshape(b + 1, 9, d)
    m_lat, m_ctx = mods[:b], mods[b:b + 1]

    w1i, w1o = _ffn_weights(ffn1_w_in[l], ffn1_w_out[l])
    w2i, w2o = _ffn_weights(ffn2_w_in[l], ffn2_w_out[l])
    wmo = mix_w_out[l].astype(BF16)
    wmi = jnp.pad(mix_w_in[l], ((0, 0), (0, LANES - MLA_ROPE))).astype(BF16)
    wuq = mla_w_uq[l].reshape(MLA_Q_RANK, heads, MLA_NOPE + MLA_ROPE)
    wuq = jnp.pad(wuq, ((0, 0), (0, 0), (0, MLA_QK_PAD - MLA_NOPE - MLA_ROPE)))
    wuq = wuq.reshape(MLA_Q_RANK, heads * MLA_QK_PAD).astype(BF16)
    wukv = mla_w_ukv[l].astype(BF16)

    x2 = x.reshape(b * n, d)
    c2 = ctx.reshape(b * nctx, d)

    x2 = _ffn(x2, m_lat[:, 0:3], norm1_g[l], w1i, w1o, rows_per_mod=n)
    c2 = _ffn(c2, m_ctx[:, 0:3], norm1_g[l], w1i, w1o, rows_per_mod=b * nctx)

    tables = _rope_tables(n)
    rq, rk, rv, rg, q, k, v = _proj(x2, m_lat[:, 3:6], norm2_g[l], wmi, mla_q_norm_g[l], wuq,
                                    mla_kv_norm_g[l], wukv, heads, b, tables)
    crk, crv, ck, cv = _proj(c2, m_ctx[:, 3:6], norm2_g[l], wmi, mla_q_norm_g[l], wuq,
                             mla_kv_norm_g[l], wukv, heads, b)
    dec = jnp.stack([ret_decay_fwd[l], ret_decay_bwd[l]]).astype(F32)
    ret = _retention(dec, rq, rk, rv, rg, crk, crv)
    mla = _mla(q, k, v, ck, cv)
    x2 = _mixout(x2, m_lat[:, 3:6], ret.reshape(b * n, -1), mla.reshape(b * n, -1), wmo,
                 rows_per_mod=n)

    x2 = _ffn(x2, m_lat[:, 6:9], norm3_g[l], w2i, w2o, rows_per_mod=n, final_g=final_norm_g)
    return x2.reshape(b, n, d)
```

```python
import functools
import math

import numpy as np

import jax
import jax.numpy as jnp
from jax import lax
from jax.experimental import pallas as pl
from jax.experimental.pallas import tpu as pltpu

GRID_W = 64
RET_DK = 64
RET_DV = 128
RET_ROPE_BASE = 10000.0
MLA_Q_RANK = 512
MLA_KV_RANK = 256
MLA_NOPE = 128
MLA_ROPE = 64
MLA_V = 128
AXIAL_BASE = 10000.0
RMS_EPS = 1e-6
GN_EPS = 1e-5

LANES = 128
MLA_QK_PAD = 2 * LANES
VMEM_LIMIT = 56 * 1024 * 1024

F32 = jnp.float32
BF16 = jnp.bfloat16


def _cparams(sem):
    return pltpu.CompilerParams(dimension_semantics=sem, vmem_limit_bytes=VMEM_LIMIT)


def _const_spec(shape):
    nd = len(shape)
    return pl.BlockSpec(shape, lambda *_: (0,) * nd, pipeline_mode=pl.Buffered(1))


def _silu(x):
    return x / (1.0 + jnp.exp(-x))


def _rms(x, g):
    return x * lax.rsqrt(jnp.mean(x * x, axis=-1, keepdims=True) + RMS_EPS) * g


def _ada_kernel(ct_ref, w_ref, b_ref, o_ref, sb_ref):
    m = sb_ref.shape[0]

    @pl.when(pl.program_id(0) == 0)
    def _():
        s = _silu(ct_ref[...])
        for r in range(m):
            sb_ref[r] = jnp.broadcast_to(s[:, r:r + 1], sb_ref.shape[1:])

    sub = 8
    col_tiles = [slice(t * LANES, (t + 1) * LANES) for t in range(w_ref.shape[1] // LANES)]

    def body(kc, accs):
        rows = pl.ds(pl.multiple_of(kc * sub, sub), sub)
        s_rows = [sb_ref[r, rows, :] for r in range(m)]
        out = []
        for t, cols in enumerate(col_tiles):
            wv = w_ref[rows, cols]
            out += [accs[t * m + r] + s_rows[r] * wv for r in range(m)]
        return tuple(out)

    zero = jnp.zeros((sub, LANES), F32)
    accs = lax.fori_loop(0, w_ref.shape[0] // sub, body, (zero,) * (m * len(col_tiles)),
                         unroll=2)
    for t, cols in enumerate(col_tiles):
        for r in range(m):
            o_ref[r:r + 1, cols] = jnp.sum(accs[t * m + r], axis=0, keepdims=True) + b_ref[:, cols]


def _adaln(cvec, ada_w, ada_b, tn_max=1024):
    m, d = cvec.shape
    n = ada_w.shape[1]
    tn = max(t for t in range(LANES, tn_max + 1, LANES) if n % t == 0)
    return pl.pallas_call(
        _ada_kernel,
        grid=(n // tn,),
        in_specs=[pl.BlockSpec((d, m), lambda j: (0, 0)),
                  pl.BlockSpec((d, tn), lambda j: (0, j)),
                  pl.BlockSpec((1, tn), lambda j: (0, j))],
        out_specs=pl.BlockSpec((m, tn), lambda j: (0, j)),
        out_shape=jax.ShapeDtypeStruct((m, n), F32),
        scratch_shapes=[pltpu.VMEM((m, d, LANES), F32)],
        compiler_params=_cparams(("arbitrary",)),
        name="adaln",
    )(cvec.T, ada_w, ada_b.reshape(1, n))


def _ffn_kernel(x_ref, mod_ref, g_ref, wi_hbm, wo_hbm, *rest, final):
    if final:
        fg_ref, o_ref, xn_ref, wi_buf, wo_buf, sem = rest
    else:
        o_ref, xn_ref, wi_buf, wo_buf, sem = rest
    i = pl.program_id(0)
    n_blocks = pl.num_programs(0)
    nf, tf = wo_hbm.shape[0], wo_hbm.shape[1]

    def copies(c, slot):
        return (pltpu.make_async_copy(wi_hbm.at[c], wi_buf.at[slot], sem.at[0, slot]),
                pltpu.make_async_copy(wo_hbm.at[c], wo_buf.at[slot], sem.at[1, slot]))

    @pl.when(i == 0)
    def _():
        for cp in copies(0, 0):
            cp.start()

    y = _rms(x_ref[...], g_ref[...])
    xn_ref[...] = (y * (1.0 + mod_ref[0, 1:2, :]) + mod_ref[0, 0:1, :]).astype(BF16)
    o_ref[...] = jnp.zeros_like(o_ref)

    def chunk(c, carry):
        step = i * nf + c
        slot = step % 2

        @pl.when(step + 1 < n_blocks * nf)
        def _():
            for cp in copies(jnp.where(c + 1 == nf, 0, c + 1), 1 - slot):
                cp.start()

        for cp in copies(c, slot):
            cp.wait()
        h = jnp.dot(xn_ref[...], wi_buf[slot], preferred_element_type=F32)
        act = (_silu(h[:, :tf]) * h[:, tf:]).astype(BF16)
        o_ref[...] += jnp.dot(act, wo_buf[slot], preferred_element_type=F32)
        return carry

    lax.fori_loop(0, nf, chunk, 0)

    out = x_ref[...] + (0.5 * mod_ref[0, 2:3, :]) * o_ref[...]
    if final:
        out = _rms(out, fg_ref[...])
    o_ref[...] = out


FFN_TF = 512


def _retile_kernel(g_ref, u_ref, o_ref):
    tf = g_ref.shape[1]
    o_ref[0, :, :tf] = g_ref[...].astype(BF16)
    o_ref[0, :, tf:] = u_ref[...].astype(BF16)


def _ffn_weights(w_in, w_out, tf=FFN_TF):
    d, f2 = w_in.shape
    nf = f2 // 2 // tf
    w = pl.pallas_call(
        _retile_kernel,
        grid=(nf,),
        in_specs=[pl.BlockSpec((d, tf), lambda j: (0, j)),
                  pl.BlockSpec((d, tf), lambda j: (0, j + nf))],
        out_specs=pl.BlockSpec((1, d, 2 * tf), lambda j: (j, 0, 0)),
        out_shape=jax.ShapeDtypeStruct((nf, d, 2 * tf), BF16),
        compiler_params=_cparams(("parallel",)),
        name="ffn_w_retile",
    )(w_in, w_in)
    return w, w_out.astype(BF16)


def _ffn(x, mod, g, w_in, w_out, rows_per_mod, final_g=None, tm=512, tf=FFN_TF):
    r, d = x.shape
    f = w_out.shape[0]
    tm = min(tm, r)
    nf = f // tf
    bpm = rows_per_mod // tm
    final = final_g is not None
    in_specs = [
        pl.BlockSpec((tm, d), lambda i: (i, 0)),
        pl.BlockSpec((1, 3, d), lambda i: (i // bpm, 0, 0)),
        pl.BlockSpec((1, d), lambda i: (0, 0)),
        pl.BlockSpec(memory_space=pl.ANY),
        pl.BlockSpec(memory_space=pl.ANY),
    ]
    args = [x, mod, g.reshape(1, d), w_in, w_out.reshape(nf, tf, d)]
    if final:
        in_specs.append(pl.BlockSpec((1, d), lambda i: (0, 0)))
        args.append(final_g.reshape(1, d))
    return pl.pallas_call(
        functools.partial(_ffn_kernel, final=final),
        grid=(r // tm,),
        in_specs=in_specs,
        out_specs=pl.BlockSpec((tm, d), lambda i: (i, 0)),
        out_shape=jax.ShapeDtypeStruct((r, d), F32),
        scratch_shapes=[pltpu.VMEM((tm, d), BF16),
                        pltpu.VMEM((2, d, 2 * tf), BF16), pltpu.VMEM((2, tf, d), BF16),
                        pltpu.SemaphoreType.DMA((2, 2))],
        compiler_params=_cparams(("arbitrary",)),
        name="ffn_final" if final else "ffn",
    )(*args)


def _swap_halves(x, half):
    lane = lax.broadcasted_iota(jnp.int32, x.shape, 1)
    first = (lane % (2 * half)) < half
    return jnp.where(first, pltpu.roll(x, LANES - half, 1), pltpu.roll(x, half, 1))


def _rope(x, cos, sin, half):
    return x * cos + _swap_halves(x, half) * sin


def _proj_kernel(x_ref, mod_ref, g_ref, w_ref, qg_ref, wuq_ref, kvg_ref, wukv_ref, *rest,
                 heads, latent):
    if latent:
        (rcos_ref, rsin_ref, acos_ref, asin_ref,
         rq_ref, rk_ref, rv_ref, rg_ref, q_ref, k_ref, v_ref) = rest
    else:
        rk_ref, rv_ref, k_ref, v_ref = rest
    hk = heads * RET_DK
    hv = heads * RET_DV
    o_rk, o_rv, o_rg = hk, 2 * hk, 2 * hk + hv
    o_cq = o_rg + hv
    o_ckv = o_cq + MLA_Q_RANK
    o_kr = o_ckv + MLA_KV_RANK

    y = _rms(x_ref[...], g_ref[...])
    h = (y * (1.0 + mod_ref[0, 1:2, :]) + mod_ref[0, 0:1, :]).astype(BF16)
    p = jnp.dot(h, w_ref[...], preferred_element_type=F32)

    for gidx in range(hk // LANES):
        sl = slice(gidx * LANES, (gidx + 1) * LANES)
        kk = p[:, o_rk + gidx * LANES:o_rk + (gidx + 1) * LANES] * (RET_DK ** -0.5)
        if latent:
            kk = _rope(kk, rcos_ref[...], rsin_ref[...], RET_DK // 2)
            qq = _rope(p[:, sl], rcos_ref[...], rsin_ref[...], RET_DK // 2)
        for t in range(LANES // RET_DK):
            hd = gidx * (LANES // RET_DK) + t
            rk_ref[0, hd] = kk[:, t * RET_DK:(t + 1) * RET_DK].astype(BF16)
            if latent:
                rq_ref[0, hd] = qq[:, t * RET_DK:(t + 1) * RET_DK].astype(BF16)
    for hd in range(heads):
        rv_ref[0, hd] = p[:, o_rv + hd * RET_DV:o_rv + (hd + 1) * RET_DV].astype(BF16)
    if latent:
        rg_ref[0] = p[:, o_rg:o_rg + hv].astype(BF16)

    kvn = _rms(p[:, o_ckv:o_ckv + MLA_KV_RANK], kvg_ref[...]).astype(BF16)
    kv = jnp.dot(kvn, wukv_ref[...], preferred_element_type=F32)
    kr = p[:, o_kr:o_kr + LANES]
    if latent:
        kr = _rope(kr, acos_ref[...], asin_ref[...], MLA_ROPE // 4)
    kr = kr.astype(BF16)
    for hd in range(heads):
        base = hd * (MLA_NOPE + MLA_V)
        k_ref[0, hd, :, 0:MLA_NOPE] = kv[:, base:base + MLA_NOPE].astype(BF16)
        k_ref[0, hd, :, MLA_NOPE:MLA_QK_PAD] = kr
        v_ref[0, hd] = kv[:, base + MLA_NOPE:base + MLA_NOPE + MLA_V].astype(BF16)
    if latent:
        qn = _rms(p[:, o_cq:o_cq + MLA_Q_RANK], qg_ref[...]).astype(BF16)
        q = jnp.dot(qn, wuq_ref[...], preferred_element_type=F32)
        scale = (MLA_NOPE + MLA_ROPE) ** -0.5 * math.log2(math.e)
        for hd in range(heads):
            base = hd * MLA_QK_PAD
            q_ref[0, hd, :, 0:MLA_NOPE] = (q[:, base:base + MLA_NOPE] * scale).astype(BF16)
            qr = _rope(q[:, base + MLA_NOPE:base + MLA_QK_PAD],
                       acos_ref[...], asin_ref[...], MLA_ROPE // 4)
            q_ref[0, hd, :, MLA_NOPE:MLA_QK_PAD] = (qr * scale).astype(BF16)


def _proj(x, mod, g, w_in_p, qg, wuq_p, kvg, wukv, heads, batch, tables=None, tm=256):
    r, d = x.shape
    n = r // batch
    tm = min(tm, n)
    nb = n // tm
    latent = tables is not None
    row = lambda i: (i, 0)
    cst2 = lambda i: (0, 0)
    in_specs = [
        pl.BlockSpec((tm, d), row),
        pl.BlockSpec((1, 3, d), lambda i: (i // nb if latent else 0, 0, 0)),
        pl.BlockSpec((1, d), cst2),
        _const_spec(w_in_p.shape),
        pl.BlockSpec((1, MLA_Q_RANK), cst2),
        _const_spec(wuq_p.shape),
        pl.BlockSpec((1, MLA_KV_RANK), cst2),
        _const_spec(wukv.shape),
    ]
    args = [x, mod, g.reshape(1, d), w_in_p, qg.reshape(1, -1), wuq_p, kvg.reshape(1, -1), wukv]
    hmaj = lambda w: pl.BlockSpec((1, heads, tm, w), lambda i: (i // nb, 0, i % nb, 0))
    hshape = lambda w: jax.ShapeDtypeStruct((batch, heads, n, w), BF16)
    if latent:
        in_specs += [pl.BlockSpec((tm, LANES), lambda i: (i % nb, 0))] * 4
        args += list(tables)
        out_specs = [hmaj(RET_DK), hmaj(RET_DK), hmaj(RET_DV),
                     pl.BlockSpec((1, tm, heads * RET_DV), lambda i: (i // nb, i % nb, 0)),
                     hmaj(MLA_QK_PAD), hmaj(MLA_QK_PAD), hmaj(MLA_V)]
        out_shape = [hshape(RET_DK), hshape(RET_DK), hshape(RET_DV),
                     jax.ShapeDtypeStruct((batch, n, heads * RET_DV), BF16),
                     hshape(MLA_QK_PAD), hshape(MLA_QK_PAD), hshape(MLA_V)]
    else:
        out_specs = [hmaj(RET_DK), hmaj(RET_DV), hmaj(MLA_QK_PAD), hmaj(MLA_V)]
        out_shape = [hshape(RET_DK), hshape(RET_DV), hshape(MLA_QK_PAD), hshape(MLA_V)]
    return pl.pallas_call(
        functools.partial(_proj_kernel, heads=heads, latent=latent),
        grid=(r // tm,),
        in_specs=in_specs,
        out_specs=out_specs,
        out_shape=out_shape,
        compiler_params=_cparams(("parallel",)),
        name="proj_lat" if latent else "proj_ctx",
    )(*args)


def _log_sigmoid(x):
    return jnp.minimum(x, 0.0) - jnp.log1p(jnp.exp(-jnp.abs(x)))


def _ret_kernel(dec_ref, q_ref, k_ref, v_ref, g_ref, ck_ref, cv_ref, o_ref, kv_ref, st_ref,
                *, chunk, unroll):
    hd = pl.program_id(1)
    n = q_ref.shape[2]
    nctx = ck_ref.shape[2]
    nc = n // chunk
    dk = RET_DK
    lgf = _log_sigmoid(jnp.full((1, 1), dec_ref[0, hd], F32))
    lgb = _log_sigmoid(jnp.full((1, 1), dec_ref[1, hd], F32))

    def col(m):
        return lax.broadcasted_iota(jnp.int32, (m, 1), 0).astype(F32)

    def weighted_kv(k, v, zf, zb):
        kf = k.astype(F32)
        kz = jnp.concatenate([kf * zf, kf * zb], axis=-1).astype(BF16)
        return lax.dot_general(kz, v, (((0,), (0,)), ((), ())), preferred_element_type=F32)

    ic = col(nctx)
    s0 = weighted_kv(ck_ref[0, 0], cv_ref[0, 0],
                     jnp.exp(lgf * (nctx - 1.0 - ic)), jnp.exp(lgb * ic))

    i = col(chunk)
    zeta_f = jnp.exp(lgf * (chunk - 1.0 - i))
    zeta_b = jnp.exp(lgb * i)
    xi_f = jnp.exp(lgf * (i + 1.0))
    xi_b = jnp.exp(lgb * (chunk - i))
    dec_f = jnp.exp(lgf * float(chunk))
    dec_b = jnp.exp(lgb * float(chunk))
    diff = i - lax.broadcasted_iota(jnp.int32, (1, chunk), 1).astype(F32)
    dmat = jnp.where(diff >= 0.0, jnp.exp(lgf * jnp.maximum(diff, 0.0)),
                     jnp.exp(lgb * jnp.maximum(-diff, 0.0)))

    def rows(c):
        return pl.ds(pl.multiple_of(c * chunk, chunk), chunk)

    def p1(c, carry):
        kv_ref[c] = weighted_kv(k_ref[0, 0, rows(c), :], v_ref[0, 0, rows(c), :], zeta_f, zeta_b)
        return carry
    lax.fori_loop(0, nc, p1, 0, unroll=unroll)

    dv = v_ref.shape[3]

    def p2f(c, s):
        st_ref[c, :, 0:dv] = s.astype(BF16)
        return dec_f * s + kv_ref[c, 0:dk, :]
    lax.fori_loop(0, nc, p2f, s0[0:dk])

    def p2b(t, s):
        c = nc - 1 - t
        st_ref[c, :, dv:2 * dv] = s.astype(BF16)
        return dec_b * s + kv_ref[c, dk:2 * dk, :]
    lax.fori_loop(0, nc, p2b, s0[dk:2 * dk])

    def p3(c, carry):
        q = q_ref[0, 0, rows(c), :]
        k = k_ref[0, 0, rows(c), :]
        v = v_ref[0, 0, rows(c), :]
        a = lax.dot_general(q, k, (((1,), (1,)), ((), ())), preferred_element_type=F32)
        y = jnp.dot((a * dmat).astype(BF16), v, preferred_element_type=F32)
        cross = jnp.dot(q, st_ref[c], preferred_element_type=F32)
        y += xi_f * cross[:, 0:dv] + xi_b * cross[:, dv:2 * dv]
        mu = jnp.mean(y, axis=-1, keepdims=True)
        yc = y - mu
        var = jnp.mean(yc * yc, axis=-1, keepdims=True)
        yn = yc * lax.rsqrt(var + GN_EPS)
        o_ref[0, rows(c), :] = (_silu(g_ref[0, rows(c), :].astype(F32)) * yn).astype(BF16)
        return carry
    lax.fori_loop(0, nc, p3, 0, unroll=unroll)


def _retention(dec, rq, rk, rv, rg, crk, crv, chunk=256, unroll=8):
    b, h, n, _ = rq.shape
    nctx = crk.shape[2]
    chunk = min(chunk, n)
    unroll = math.gcd(unroll, n // chunk)
    hm = lambda w, rows: pl.BlockSpec((1, 1, rows, w), lambda bi, hi: (bi, hi, 0, 0))
    return pl.pallas_call(
        functools.partial(_ret_kernel, chunk=chunk, unroll=unroll),
        grid=(b, h),
        in_specs=[pl.BlockSpec(memory_space=pltpu.SMEM),
                  hm(RET_DK, n), hm(RET_DK, n), hm(RET_DV, n),
                  pl.BlockSpec((1, n, RET_DV), lambda bi, hi: (bi, 0, hi)),
                  hm(RET_DK, nctx), hm(RET_DV, nctx)],
        out_specs=pl.BlockSpec((1, n, RET_DV), lambda bi, hi: (bi, 0, hi)),
        out_shape=jax.ShapeDtypeStruct((b, n, h * RET_DV), BF16),
        scratch_shapes=[pltpu.VMEM((n // chunk, 2 * RET_DK, RET_DV), F32),
                        pltpu.VMEM((n // chunk, RET_DK, 2 * RET_DV), BF16)],
        compiler_params=_cparams(("parallel", "parallel")),
        name="retention",
    )(dec, rq, rk, rv, rg, crk, crv)


def _mla_kernel(q_ref, k_ref, vp_ref, v_ref, ck_ref, cv_ref, o_ref,
                m_ref, acc_ref, s_ref, *, sub):
    j = pl.program_id(3)
    nj = pl.num_programs(3)
    nt = (((1,), (1,)), ((), ()))
    n_sub = q_ref.shape[2] // sub
    row_tiles = [slice(r * sub, (r + 1) * sub) for r in range(n_sub)]
    last = n_sub - 1

    def with_ones(v):
        return jnp.concatenate([v, jnp.ones_like(v)], axis=-1)

    def softmax_update(rows, s):
        m_old = m_ref[rows, :]
        m_new = jnp.maximum(m_old, jnp.max(s, axis=-1, keepdims=True))
        m_ref[rows, :] = m_new
        p = jnp.exp2(s - jnp.tile(m_new, (1, s.shape[1] // LANES))).astype(BF16)
        return jnp.exp2(m_old - m_new), p

    def accumulate(rows, alpha, p, v1):
        acc_ref[rows, :] = jnp.tile(alpha, (1, 2)) * acc_ref[rows, :] + jnp.dot(
            p, v1, preferred_element_type=F32)

    @pl.when(j == 0)
    def _():
        s = lax.dot_general(q_ref[0, 0], ck_ref[0, 0], nt, preferred_element_type=F32)
        m = jnp.max(s, axis=-1, keepdims=True)
        p = jnp.exp2(s - m)
        m_ref[...] = jnp.broadcast_to(m, m_ref.shape)
        acc_ref[...] = jnp.dot(p.astype(BF16), with_ones(cv_ref[0, 0]),
                               preferred_element_type=F32)
        s_ref[...] = jnp.full_like(s_ref, -jnp.inf)

    k = k_ref[0, 0]
    v_cur = with_ones(v_ref[0, 0])
    scores, alphas, probs = {}, {}, {}

    def stage_scores(r):
        scores[r] = lax.dot_general(q_ref[0, 0, row_tiles[r], :], k, nt,
                                    preferred_element_type=F32)

    def stage_softmax(r):
        alphas[r], probs[r] = softmax_update(row_tiles[r], scores.pop(r))

    def stage_values(r):
        accumulate(row_tiles[r], alphas.pop(r), probs.pop(r), v_cur)

    stage_scores(0)
    carried = softmax_update(row_tiles[last], s_ref[...])
    for t in range(1, n_sub + 1):
        if t < n_sub:
            stage_scores(t)
        if t - 1 < last:
            stage_softmax(t - 1)
        if t == 1:
            accumulate(row_tiles[last], *carried, with_ones(vp_ref[0, 0]))
        if 0 <= t - 2:
            stage_values(t - 2)
    s_ref[...] = scores.pop(last)

    @pl.when(j == nj - 1)
    def _():
        accumulate(row_tiles[last], *softmax_update(row_tiles[last], s_ref[...]), v_cur)
        acc = acc_ref[...]
        o_ref[0] = (acc[:, :MLA_V] / acc[:, MLA_V:]).astype(BF16)


def _mla(q, k, v, ck, cv, tq=4096, tk=2048, sub=512):
    b, h, n, w = q.shape
    nctx = ck.shape[2]
    tq = min(tq, n)
    tk = min(tk, n)
    sub = min(sub, tq)
    return pl.pallas_call(
        functools.partial(_mla_kernel, sub=sub),
        grid=(b, h, n // tq, n // tk),
        in_specs=[pl.BlockSpec((1, 1, tq, w), lambda bi, hi, i, j: (bi, hi, i, 0)),
                  pl.BlockSpec((1, 1, tk, w), lambda bi, hi, i, j: (bi, hi, j, 0)),
                  pl.BlockSpec((1, 1, tk, MLA_V),
                               lambda bi, hi, i, j: (bi, hi, jnp.maximum(j - 1, 0), 0)),
                  pl.BlockSpec((1, 1, tk, MLA_V), lambda bi, hi, i, j: (bi, hi, j, 0)),
                  pl.BlockSpec((1, 1, nctx, w), lambda bi, hi, i, j: (bi, hi, 0, 0)),
                  pl.BlockSpec((1, 1, nctx, MLA_V), lambda bi, hi, i, j: (bi, hi, 0, 0))],
        out_specs=pl.BlockSpec((1, tq, MLA_V), lambda bi, hi, i, j: (bi, i, hi)),
        out_shape=jax.ShapeDtypeStruct((b, n, h * MLA_V), BF16),
        scratch_shapes=[pltpu.VMEM((tq, LANES), F32), pltpu.VMEM((tq, 2 * MLA_V), F32),
                        pltpu.VMEM((sub, tk), F32)],
        compiler_params=_cparams(("parallel", "parallel", "parallel", "arbitrary")),
        name="mla_attn",
    )(q, k, v, v, ck, cv)


def _mixout_kernel(x_ref, mod_ref, ret_ref, mla_ref, w_ref, o_ref):
    hr = ret_ref.shape[1]
    mix = jnp.dot(ret_ref[...], w_ref[0:hr, :], preferred_element_type=F32)
    mix += jnp.dot(mla_ref[...], w_ref[hr:, :], preferred_element_type=F32)
    o_ref[...] = x_ref[...] + mod_ref[0, 2:3, :] * mix


def _mixout(x, mod, ret, mla, w, rows_per_mod, tm=512):
    r, d = x.shape
    tm = min(tm, r)
    bpm = rows_per_mod // tm
    return pl.pallas_call(
        _mixout_kernel,
        grid=(r // tm,),
        in_specs=[pl.BlockSpec((tm, d), lambda i: (i, 0)),
                  pl.BlockSpec((1, 3, d), lambda i: (i // bpm, 0, 0)),
                  pl.BlockSpec((tm, ret.shape[1]), lambda i: (i, 0)),
                  pl.BlockSpec((tm, mla.shape[1]), lambda i: (i, 0)),
                  _const_spec(w.shape)],
        out_specs=pl.BlockSpec((tm, d), lambda i: (i, 0)),
        out_shape=jax.ShapeDtypeStruct((r, d), F32),
        compiler_params=_cparams(("parallel",)),
        name="mixout",
    )(x, mod, ret, mla, w)


def _rope_tables(n):
    pos = np.arange(n, dtype=np.float64)
    inv_r = RET_ROPE_BASE ** (-np.arange(0, RET_DK, 2, dtype=np.float64) / RET_DK)
    ang = pos[:, None] * inv_r[None, :]
    c, s = np.cos(ang), np.sin(ang)
    rcos = np.tile(np.concatenate([c, c], -1), (1, LANES // RET_DK))
    rsin = np.tile(np.concatenate([-s, s], -1), (1, LANES // RET_DK))
    ax = MLA_ROPE // 2
    inv_a = AXIAL_BASE ** (-np.arange(0, ax, 2, dtype=np.float64) / ax)
    pr = (np.arange(n) // GRID_W).astype(np.float64)[:, None] * inv_a[None, :]
    pc = (np.arange(n) % GRID_W).astype(np.float64)[:, None] * inv_a[None, :]
    cr, sr, cc, sc = np.cos(pr), np.sin(pr), np.cos(pc), np.sin(pc)
    pad1 = np.ones((n, LANES - MLA_ROPE))
    pad0 = np.zeros((n, LANES - MLA_ROPE))
    acos = np.concatenate([cr, cr, cc, cc, pad1], -1)
    asin = np.concatenate([-sr, sr, -sc, sc, pad0], -1)
    return tuple(jnp.asarray(t.astype(np.float32)) for t in (rcos, rsin, acos, asin))


def kernel(x, c, ctx, c_ctx, ada_w, ada_b, norm1_g, ffn1_w_in, ffn1_w_out, norm2_g, mix_w_in,
           ret_decay_fwd, ret_decay_bwd, mla_q_norm_g, mla_w_uq, mla_kv_norm_g, mla_w_ukv,
           mix_w_out, norm3_g, ffn2_w_in, ffn2_w_out, final_norm_g):
    b, n, d = x.shape
    nctx = ctx.shape[1]
    depth = ada_w.shape[0]
    heads = ret_decay_fwd.shape[1]
    assert depth == 1, "single-layer block"
    assert mla_w_uq.shape[2] == heads * (MLA_NOPE + MLA_ROPE)
    l = 0

    cvec = jnp.concatenate([c, c_ctx[None, :]], 0)
    mods = _adaln(cvec, ada_w[l], ada_b[l]).reshape(b + 1, 9, d)
    m_lat, m_ctx = mods[:b], mods[b:b + 1]

    w1i, w1o = _ffn_weights(ffn1_w_in[l], ffn1_w_out[l])
    w2i, w2o = _ffn_weights(ffn2_w_in[l], ffn2_w_out[l])
    wmo = mix_w_out[l].astype(BF16)
    wmi = jnp.pad(mix_w_in[l], ((0, 0), (0, LANES - MLA_ROPE))).astype(BF16)
    wuq = mla_w_uq[l].reshape(MLA_Q_RANK, heads, MLA_NOPE + MLA_ROPE)
    wuq = jnp.pad(wuq, ((0, 0), (0, 0), (0, MLA_QK_PAD - MLA_NOPE - MLA_ROPE)))
    wuq = wuq.reshape(MLA_Q_RANK, heads * MLA_QK_PAD).astype(BF16)
    wukv = mla_w_ukv[l].astype(BF16)

    x2 = x.reshape(b * n, d)
    c2 = ctx.reshape(b * nctx, d)

    x2 = _ffn(x2, m_lat[:, 0:3], norm1_g[l], w1i, w1o, rows_per_mod=n)
    c2 = _ffn(c2, m_ctx[:, 0:3], norm1_g[l], w1i, w1o, rows_per_mod=b * nctx)

    tables = _rope_tables(n)
    rq, rk, rv, rg, q, k, v = _proj(x2, m_lat[:, 3:6], norm2_g[l], wmi, mla_q_norm_g[l], wuq,
                                    mla_kv_norm_g[l], wukv, heads, b, tables)
    crk, crv, ck, cv = _proj(c2, m_ctx[:, 3:6], norm2_g[l], wmi, mla_q_norm_g[l], wuq,
                             mla_kv_norm_g[l], wukv, heads, b)
    dec = jnp.stack([ret_decay_fwd[l], ret_decay_bwd[l]]).astype(F32)
    ret = _retention(dec, rq, rk, rv, rg, crk, crv)
    mla = _mla(q, k, v, ck, cv)
    x2 = _mixout(x2, m_lat[:, 3:6], ret.reshape(b * n, -1), mla.reshape(b * n, -1), wmo,
                 rows_per_mod=n)

    x2 = _ffn(x2, m_lat[:, 6:9], norm3_g[l], w2i, w2o, rows_per_mod=n, final_g=final_norm_g)
    return x2.reshape(b, n, d)
```

```python
import functools
import math

import numpy as np

import jax
import jax.numpy as jnp
from jax import lax
from jax.experimental import pallas as pl
from jax.experimental.pallas import tpu as pltpu

GRID_W = 64
RET_DK = 64
RET_DV = 128
RET_ROPE_BASE = 10000.0
MLA_Q_RANK = 512
MLA_KV_RANK = 256
MLA_NOPE = 128
MLA_ROPE = 64
MLA_V = 128
AXIAL_BASE = 10000.0
RMS_EPS = 1e-6
GN_EPS = 1e-5

LANES = 128
MLA_QK_PAD = 2 * LANES
VMEM_LIMIT = 56 * 1024 * 1024

F32 = jnp.float32
BF16 = jnp.bfloat16


def _cparams(sem):
    return pltpu.CompilerParams(dimension_semantics=sem, vmem_limit_bytes=VMEM_LIMIT)


def _const_spec(shape):
    nd = len(shape)
    return pl.BlockSpec(shape, lambda *_: (0,) * nd, pipeline_mode=pl.Buffered(1))


def _silu(x):
    return x / (1.0 + jnp.exp(-x))


def _rms(x, g):
    return x * lax.rsqrt(jnp.mean(x * x, axis=-1, keepdims=True) + RMS_EPS) * g


def _ada_kernel(ct_ref, w_ref, b_ref, o_ref, sb_ref):
    m = sb_ref.shape[0]

    @pl.when(pl.program_id(0) == 0)
    def _():
        s = _silu(ct_ref[...])
        for r in range(m):
            sb_ref[r] = jnp.broadcast_to(s[:, r:r + 1], sb_ref.shape[1:])

    sub = 8
    col_tiles = [slice(t * LANES, (t + 1) * LANES) for t in range(w_ref.shape[1] // LANES)]

    def body(kc, accs):
        rows = pl.ds(pl.multiple_of(kc * sub, sub), sub)
        s_rows = [sb_ref[r, rows, :] for r in range(m)]
        out = []
        for t, cols in enumerate(col_tiles):
            wv = w_ref[rows, cols]
            out += [accs[t * m + r] + s_rows[r] * wv for r in range(m)]
        return tuple(out)

    zero = jnp.zeros((sub, LANES), F32)
    accs = lax.fori_loop(0, w_ref.shape[0] // sub, body, (zero,) * (m * len(col_tiles)),
                         unroll=4)
    for t, cols in enumerate(col_tiles):
        for r in range(m):
            o_ref[r:r + 1, cols] = jnp.sum(accs[t * m + r], axis=0, keepdims=True) + b_ref[:, cols]


def _adaln(cvec, ada_w, ada_b, tn_max=1024):
    m, d = cvec.shape
    n = ada_w.shape[1]
    tn = max(t for t in range(LANES, tn_max + 1, LANES) if n % t == 0)
    return pl.pallas_call(
        _ada_kernel,
        grid=(n // tn,),
        in_specs=[pl.BlockSpec((d, m), lambda j: (0, 0)),
                  pl.BlockSpec((d, tn), lambda j: (0, j)),
                  pl.BlockSpec((1, tn), lambda j: (0, j))],
        out_specs=pl.BlockSpec((m, tn), lambda j: (0, j)),
        out_shape=jax.ShapeDtypeStruct((m, n), F32),
        scratch_shapes=[pltpu.VMEM((m, d, LANES), F32)],
        compiler_params=_cparams(("arbitrary",)),
        name="adaln",
    )(cvec.T, ada_w, ada_b.reshape(1, n))


def _ffn_kernel(x_ref, mod_ref, g_ref, wi_hbm, wo_hbm, *rest, final):
    if final:
        fg_ref, o_ref, xn_ref, wi_buf, wo_buf, sem = rest
    else:
        o_ref, xn_ref, wi_buf, wo_buf, sem = rest
    i = pl.program_id(0)
    n_blocks = pl.num_programs(0)
    nf, tf = wo_hbm.shape[0], wo_hbm.shape[1]

    def copies(c, slot):
        return (pltpu.make_async_copy(wi_hbm.at[c], wi_buf.at[slot], sem.at[0, slot]),
                pltpu.make_async_copy(wo_hbm.at[c], wo_buf.at[slot], sem.at[1, slot]))

    @pl.when(i == 0)
    def _():
        for cp in copies(0, 0):
            cp.start()

    y = _rms(x_ref[...], g_ref[...])
    xn_ref[...] = (y * (1.0 + mod_ref[0, 1:2, :]) + mod_ref[0, 0:1, :]).astype(BF16)
    o_ref[...] = jnp.zeros_like(o_ref)

    def chunk(c, carry):
        step = i * nf + c
        slot = step % 2

        @pl.when(step + 1 < n_blocks * nf)
        def _():
            for cp in copies(jnp.where(c + 1 == nf, 0, c + 1), 1 - slot):
                cp.start()

        for cp in copies(c, slot):
            cp.wait()
        h = jnp.dot(xn_ref[...], wi_buf[slot], preferred_element_type=F32)
        act = (_silu(h[:, :tf]) * h[:, tf:]).astype(BF16)
        o_ref[...] += jnp.dot(act, wo_buf[slot], preferred_element_type=F32)
        return carry

    lax.fori_loop(0, nf, chunk, 0)

    out = x_ref[...] + (0.5 * mod_ref[0, 2:3, :]) * o_ref[...]
    if final:
        out = _rms(out, fg_ref[...])
    o_ref[...] = out


FFN_TF = 512


def _retile_kernel(g_ref, u_ref, o_ref):
    tf = g_ref.shape[1]
    o_ref[0, :, :tf] = g_ref[...].astype(BF16)
    o_ref[0, :, tf:] = u_ref[...].astype(BF16)


def _ffn_weights(w_in, w_out, tf=FFN_TF):
    d, f2 = w_in.shape
    nf = f2 // 2 // tf
    w = pl.pallas_call(
        _retile_kernel,
        grid=(nf,),
        in_specs=[pl.BlockSpec((d, tf), lambda j: (0, j)),
                  pl.BlockSpec((d, tf), lambda j: (0, j + nf))],
        out_specs=pl.BlockSpec((1, d, 2 * tf), lambda j: (j, 0, 0)),
        out_shape=jax.ShapeDtypeStruct((nf, d, 2 * tf), BF16),
        compiler_params=_cparams(("parallel",)),
        name="ffn_w_retile",
    )(w_in, w_in)
    return w, w_out.astype(BF16)


def _ffn(x, mod, g, w_in, w_out, rows_per_mod, final_g=None, tm=512, tf=FFN_TF):
    r, d = x.shape
    f = w_out.shape[0]
    tm = min(tm, r)
    nf = f // tf
    bpm = rows_per_mod // tm
    final = final_g is not None
    in_specs = [
        pl.BlockSpec((tm, d), lambda i: (i, 0)),
        pl.BlockSpec((1, 3, d), lambda i: (i // bpm, 0, 0)),
        pl.BlockSpec((1, d), lambda i: (0, 0)),
        pl.BlockSpec(memory_space=pl.ANY),
        pl.BlockSpec(memory_space=pl.ANY),
    ]
    args = [x, mod, g.reshape(1, d), w_in, w_out.reshape(nf, tf, d)]
    if final:
        in_specs.append(pl.BlockSpec((1, d), lambda i: (0, 0)))
        args.append(final_g.reshape(1, d))
    return pl.pallas_call(
        functools.partial(_ffn_kernel, final=final),
        grid=(r // tm,),
        in_specs=in_specs,
        out_specs=pl.BlockSpec((tm, d), lambda i: (i, 0)),
        out_shape=jax.ShapeDtypeStruct((r, d), F32),
        scratch_shapes=[pltpu.VMEM((tm, d), BF16),
                        pltpu.VMEM((2, d, 2 * tf), BF16), pltpu.VMEM((2, tf, d), BF16),
                        pltpu.SemaphoreType.DMA((2, 2))],
        compiler_params=_cparams(("arbitrary",)),
        name="ffn_final" if final else "ffn",
    )(*args)


def _swap_halves(x, half):
    lane = lax.broadcasted_iota(jnp.int32, x.shape, 1)
    first = (lane % (2 * half)) < half
    return jnp.where(first, pltpu.roll(x, LANES - half, 1), pltpu.roll(x, half, 1))


def _rope(x, cos, sin, half):
    return x * cos + _swap_halves(x, half) * sin


def _proj_kernel(x_ref, mod_ref, g_ref, w_ref, qg_ref, wuq_ref, kvg_ref, wukv_ref, *rest,
                 heads, latent):
    if latent:
        (rcos_ref, rsin_ref, acos_ref, asin_ref,
         rq_ref, rk_ref, rv_ref, rg_ref, q_ref, k_ref, v_ref) = rest
    else:
        rk_ref, rv_ref, k_ref, v_ref = rest
    hk = heads * RET_DK
    hv = heads * RET_DV
    o_rk, o_rv, o_rg = hk, 2 * hk, 2 * hk + hv
    o_cq = o_rg + hv
    o_ckv = o_cq + MLA_Q_RANK
    o_kr = o_ckv + MLA_KV_RANK

    y = _rms(x_ref[...], g_ref[...])
    h = (y * (1.0 + mod_ref[0, 1:2, :]) + mod_ref[0, 0:1, :]).astype(BF16)
    p = jnp.dot(h, w_ref[...], preferred_element_type=F32)

    for gidx in range(hk // LANES):
        sl = slice(gidx * LANES, (gidx + 1) * LANES)
        kk = p[:, o_rk + gidx * LANES:o_rk + (gidx + 1) * LANES] * (RET_DK ** -0.5)
        if latent:
            kk = _rope(kk, rcos_ref[...], rsin_ref[...], RET_DK // 2)
            qq = _rope(p[:, sl], rcos_ref[...], rsin_ref[...], RET_DK // 2)
        for t in range(LANES // RET_DK):
            hd = gidx * (LANES // RET_DK) + t
            rk_ref[0, hd] = kk[:, t * RET_DK:(t + 1) * RET_DK].astype(BF16)
            if latent:
                rq_ref[0, hd] = qq[:, t * RET_DK:(t + 1) * RET_DK].astype(BF16)
    for hd in range(heads):
        rv_ref[0, hd] = p[:, o_rv + hd * RET_DV:o_rv + (hd + 1) * RET_DV].astype(BF16)
    if latent:
        rg_ref[0] = p[:, o_rg:o_rg + hv].astype(BF16)

    kvn = _rms(p[:, o_ckv:o_ckv + MLA_KV_RANK], kvg_ref[...]).astype(BF16)
    kv = jnp.dot(kvn, wukv_ref[...], preferred_element_type=F32)
    kr = p[:, o_kr:o_kr + MLA_ROPE]
    kr = jnp.concatenate([kr, jnp.zeros((kr.shape[0], LANES - MLA_ROPE), F32)], axis=-1)
    if latent:
        kr = _rope(kr, acos_ref[...], asin_ref[...], MLA_ROPE // 4)
    kr = kr.astype(BF16)
    for hd in range(heads):
        base = hd * (MLA_NOPE + MLA_V)
        k_ref[0, hd, :, 0:MLA_NOPE] = kv[:, base:base + MLA_NOPE].astype(BF16)
        k_ref[0, hd, :, MLA_NOPE:MLA_QK_PAD] = kr
        v_ref[0, hd] = kv[:, base + MLA_NOPE:base + MLA_NOPE + MLA_V].astype(BF16)
    if latent:
        qn = _rms(p[:, o_cq:o_cq + MLA_Q_RANK], qg_ref[...]).astype(BF16)
        q = jnp.dot(qn, wuq_ref[...], preferred_element_type=F32)
        scale = (MLA_NOPE + MLA_ROPE) ** -0.5 * math.log2(math.e)
        for hd in range(heads):
            base = hd * MLA_QK_PAD
            q_ref[0, hd, :, 0:MLA_NOPE] = (q[:, base:base + MLA_NOPE] * scale).astype(BF16)
            qr = _rope(q[:, base + MLA_NOPE:base + MLA_QK_PAD],
                       acos_ref[...], asin_ref[...], MLA_ROPE // 4)
            q_ref[0, hd, :, MLA_NOPE:MLA_QK_PAD] = (qr * scale).astype(BF16)


def _proj(x, mod, g, w_in_p, qg, wuq_p, kvg, wukv, heads, batch, tables=None, tm=256):
    r, d = x.shape
    n = r // batch
    tm = min(tm, n)
    nb = n // tm
    latent = tables is not None
    row = lambda i: (i, 0)
    cst2 = lambda i: (0, 0)
    in_specs = [
        pl.BlockSpec((tm, d), row),
        pl.BlockSpec((1, 3, d), lambda i: (i // nb if latent else 0, 0, 0)),
        pl.BlockSpec((1, d), cst2),
        _const_spec(w_in_p.shape),
        pl.BlockSpec((1, MLA_Q_RANK), cst2),
        _const_spec(wuq_p.shape),
        pl.BlockSpec((1, MLA_KV_RANK), cst2),
        _const_spec(wukv.shape),
    ]
    args = [x, mod, g.reshape(1, d), w_in_p, qg.reshape(1, -1), wuq_p, kvg.reshape(1, -1), wukv]
    hmaj = lambda w: pl.BlockSpec((1, heads, tm, w), lambda i: (i // nb, 0, i % nb, 0))
    hshape = lambda w: jax.ShapeDtypeStruct((batch, heads, n, w), BF16)
    if latent:
        in_specs += [pl.BlockSpec((tm, LANES), lambda i: (i % nb, 0))] * 4
        args += list(tables)
        out_specs = [hmaj(RET_DK), hmaj(RET_DK), hmaj(RET_DV),
                     pl.BlockSpec((1, tm, heads * RET_DV), lambda i: (i // nb, i % nb, 0)),
                     hmaj(MLA_QK_PAD), hmaj(MLA_QK_PAD), hmaj(MLA_V)]
        out_shape = [hshape(RET_DK), hshape(RET_DK), hshape(RET_DV),
                     jax.ShapeDtypeStruct((batch, n, heads * RET_DV), BF16),
                     hshape(MLA_QK_PAD), hshape(MLA_QK_PAD), hshape(MLA_V)]
    else:
        out_specs = [hmaj(RET_DK), hmaj(RET_DV), hmaj(MLA_QK_PAD), hmaj(MLA_V)]
        out_shape = [hshape(RET_DK), hshape(RET_DV), hshape(MLA_QK_PAD), hshape(MLA_V)]
    return pl.pallas_call(
        functools.partial(_proj_kernel, heads=heads, latent=latent),
        grid=(r // tm,),
        in_specs=in_specs,
        out_specs=out_specs,
        out_shape=out_shape,
        compiler_params=_cparams(("parallel",)),
        name="proj_lat" if latent else "proj_ctx",
    )(*args)


def _log_sigmoid(x):
    return jnp.minimum(x, 0.0) - jnp.log1p(jnp.exp(-jnp.abs(x)))


def _ret_kernel(dec_ref, q_ref, k_ref, v_ref, g_ref, ck_ref, cv_ref, o_ref, kv_ref, st_ref,
                *, chunk, unroll):
    hd = pl.program_id(1)
    n = q_ref.shape[2]
    nctx = ck_ref.shape[2]
    nc = n // chunk
    dk = RET_DK
    lgf = _log_sigmoid(jnp.full((1, 1), dec_ref[0, hd], F32))
    lgb = _log_sigmoid(jnp.full((1, 1), dec_ref[1, hd], F32))

    def col(m):
        return lax.broadcasted_iota(jnp.int32, (m, 1), 0).astype(F32)

    def weighted_kv(k, v, zf, zb):
        kf = k.astype(F32)
        kz = jnp.concatenate([kf * zf, kf * zb], axis=-1).astype(BF16)
        return lax.dot_general(kz, v, (((0,), (0,)), ((), ())), preferred_element_type=F32)

    ic = col(nctx)
    s0 = weighted_kv(ck_ref[0, 0], cv_ref[0, 0],
                     jnp.exp(lgf * (nctx - 1.0 - ic)), jnp.exp(lgb * ic))

    i = col(chunk)
    zeta_f = jnp.exp(lgf * (chunk - 1.0 - i))
    zeta_b = jnp.exp(lgb * i)
    xi_f = jnp.exp(lgf * (i + 1.0))
    xi_b = jnp.exp(lgb * (chunk - i))
    dec_f = jnp.exp(lgf * float(chunk))
    dec_b = jnp.exp(lgb * float(chunk))
    diff = i - lax.broadcasted_iota(jnp.int32, (1, chunk), 1).astype(F32)
    dmat = jnp.where(diff >= 0.0, jnp.exp(lgf * jnp.maximum(diff, 0.0)),
                     jnp.exp(lgb * jnp.maximum(-diff, 0.0)))

    def rows(c):
        return pl.ds(pl.multiple_of(c * chunk, chunk), chunk)

    def p1(c, carry):
        kv_ref[c] = weighted_kv(k_ref[0, 0, rows(c), :], v_ref[0, 0, rows(c), :], zeta_f, zeta_b)
        return carry
    lax.fori_loop(0, nc, p1, 0, unroll=unroll)

    dv = v_ref.shape[3]

    def p2f(c, s):
        st_ref[c, :, 0:dv] = s.astype(BF16)
        return dec_f * s + kv_ref[c, 0:dk, :]
    lax.fori_loop(0, nc, p2f, s0[0:dk])

    def p2b(t, s):
        c = nc - 1 - t
        st_ref[c, :, dv:2 * dv] = s.astype(BF16)
        return dec_b * s + kv_ref[c, dk:2 * dk, :]
    lax.fori_loop(0, nc, p2b, s0[dk:2 * dk])

    def p3(c, carry):
        q = q_ref[0, 0, rows(c), :]
        k = k_ref[0, 0, rows(c), :]
        v = v_ref[0, 0, rows(c), :]
        a = lax.dot_general(q, k, (((1,), (1,)), ((), ())), preferred_element_type=F32)
        y = jnp.dot((a * dmat).astype(BF16), v, preferred_element_type=F32)
        cross = jnp.dot(q, st_ref[c], preferred_element_type=F32)
        y += xi_f * cross[:, 0:dv] + xi_b * cross[:, dv:2 * dv]
        mu = jnp.mean(y, axis=-1, keepdims=True)
        yc = y - mu
        var = jnp.mean(yc * yc, axis=-1, keepdims=True)
        yn = yc * lax.rsqrt(var + GN_EPS)
        o_ref[0, rows(c), :] = (_silu(g_ref[0, rows(c), :].astype(F32)) * yn).astype(BF16)
        return carry
    lax.fori_loop(0, nc, p3, 0, unroll=unroll)


def _retention(dec, rq, rk, rv, rg, crk, crv, chunk=256, unroll=8):
    b, h, n, _ = rq.shape
    nctx = crk.shape[2]
    chunk = min(chunk, n)
    unroll = math.gcd(unroll, n // chunk)
    hm = lambda w, rows: pl.BlockSpec((1, 1, rows, w), lambda bi, hi: (bi, hi, 0, 0))
    return pl.pallas_call(
        functools.partial(_ret_kernel, chunk=chunk, unroll=unroll),
        grid=(b, h),
        in_specs=[pl.BlockSpec(memory_space=pltpu.SMEM),
                  hm(RET_DK, n), hm(RET_DK, n), hm(RET_DV, n),
                  pl.BlockSpec((1, n, RET_DV), lambda bi, hi: (bi, 0, hi)),
                  hm(RET_DK, nctx), hm(RET_DV, nctx)],
        out_specs=pl.BlockSpec((1, n, RET_DV), lambda bi, hi: (bi, 0, hi)),
        out_shape=jax.ShapeDtypeStruct((b, n, h * RET_DV), BF16),
        scratch_shapes=[pltpu.VMEM((n // chunk, 2 * RET_DK, RET_DV), F32),
                        pltpu.VMEM((n // chunk, RET_DK, 2 * RET_DV), BF16)],
        compiler_params=_cparams(("parallel", "parallel")),
        name="retention",
    )(dec, rq, rk, rv, rg, crk, crv)


def _mla_kernel(q_ref, k_ref, vp_ref, v_ref, ck_ref, cv_ref, o_ref,
                m_ref, acc_ref, s_ref, *, sub):
    j = pl.program_id(3)
    nj = pl.num_programs(3)
    nt = (((1,), (1,)), ((), ()))
    n_sub = q_ref.shape[2] // sub
    row_tiles = [slice(r * sub, (r + 1) * sub) for r in range(n_sub)]
    last = n_sub - 1

    def with_ones(v):
        return jnp.concatenate([v, jnp.ones_like(v)], axis=-1)

    def softmax_update(rows, s):
        m_old = m_ref[rows, :]
        m_new = jnp.maximum(m_old, jnp.max(s, axis=-1, keepdims=True))
        m_ref[rows, :] = m_new
        p = jnp.exp2(s - jnp.tile(m_new, (1, s.shape[1] // LANES))).astype(BF16)
        return jnp.exp2(m_old - m_new), p

    def accumulate(rows, alpha, p, v1):
        acc_ref[rows, :] = jnp.tile(alpha, (1, 2)) * acc_ref[rows, :] + jnp.dot(
            p, v1, preferred_element_type=F32)

    @pl.when(j == 0)
    def _():
        s = lax.dot_general(q_ref[0, 0], ck_ref[0, 0], nt, preferred_element_type=F32)
        m = jnp.max(s, axis=-1, keepdims=True)
        p = jnp.exp2(s - m)
        m_ref[...] = jnp.broadcast_to(m, m_ref.shape)
        acc_ref[...] = jnp.dot(p.astype(BF16), with_ones(cv_ref[0, 0]),
                               preferred_element_type=F32)
        s_ref[...] = jnp.full_like(s_ref, -jnp.inf)

    k = k_ref[0, 0]
    v_cur = with_ones(v_ref[0, 0])
    scores, alphas, probs = {}, {}, {}

    def stage_scores(r):
        scores[r] = lax.dot_general(q_ref[0, 0, row_tiles[r], :], k, nt,
                                    preferred_element_type=F32)

    def stage_softmax(r):
        alphas[r], probs[r] = softmax_update(row_tiles[r], scores.pop(r))

    def stage_values(r):
        accumulate(row_tiles[r], alphas.pop(r), probs.pop(r), v_cur)

    stage_scores(0)
    carried = softmax_update(row_tiles[last], s_ref[...])
    for t in range(1, n_sub + 1):
        if t < n_sub:
            stage_scores(t)
        if t - 1 < last:
            stage_softmax(t - 1)
        if t == 1:
            accumulate(row_tiles[last], *carried, with_ones(vp_ref[0, 0]))
        if 0 <= t - 2:
            stage_values(t - 2)
    s_ref[...] = scores.pop(last)

    @pl.when(j == nj - 1)
    def _():
        accumulate(row_tiles[last], *softmax_update(row_tiles[last], s_ref[...]), v_cur)
        acc = acc_ref[...]
        o_ref[0] = (acc[:, :MLA_V] / acc[:, MLA_V:]).astype(BF16)


def _mla(q, k, v, ck, cv, tq=4096, tk=2048, sub=512):
    b, h, n, w = q.shape
    nctx = ck.shape[2]
    tq = min(tq, n)
    tk = min(tk, n)
    sub = min(sub, tq)
    return pl.pallas_call(
        functools.partial(_mla_kernel, sub=sub),
        grid=(b, h, n // tq, n // tk),
        in_specs=[pl.BlockSpec((1, 1, tq, w), lambda bi, hi, i, j: (bi, hi, i, 0)),
                  pl.BlockSpec((1, 1, tk, w), lambda bi, hi, i, j: (bi, hi, j, 0)),
                  pl.BlockSpec((1, 1, tk, MLA_V),
                               lambda bi, hi, i, j: (bi, hi, jnp.maximum(j - 1, 0), 0)),
                  pl.BlockSpec((1, 1, tk, MLA_V), lambda bi, hi, i, j: (bi, hi, j, 0)),
                  pl.BlockSpec((1, 1, nctx, w), lambda bi, hi, i, j: (bi, hi, 0, 0)),
                  pl.BlockSpec((1, 1, nctx, MLA_V), lambda bi, hi, i, j: (bi, hi, 0, 0))],
        out_specs=pl.BlockSpec((1, tq, MLA_V), lambda bi, hi, i, j: (bi, i, hi)),
        out_shape=jax.ShapeDtypeStruct((b, n, h * MLA_V), BF16),
        scratch_shapes=[pltpu.VMEM((tq, LANES), F32), pltpu.VMEM((tq, 2 * MLA_V), F32),
                        pltpu.VMEM((sub, tk), F32)],
        compiler_params=_cparams(("parallel", "parallel", "parallel", "arbitrary")),
        name="mla_attn",
    )(q, k, v, v, ck, cv)


def _mixout_kernel(x_ref, mod_ref, ret_ref, mla_ref, w_ref, o_ref):
    hr = ret_ref.shape[1]
    mix = jnp.dot(ret_ref[...], w_ref[0:hr, :], preferred_element_type=F32)
    mix += jnp.dot(mla_ref[...], w_ref[hr:, :], preferred_element_type=F32)
    o_ref[...] = x_ref[...] + mod_ref[0, 2:3, :] * mix


def _mixout(x, mod, ret, mla, w, rows_per_mod, tm=512):
    r, d = x.shape
    tm = min(tm, r)
    bpm = rows_per_mod // tm
    return pl.pallas_call(
        _mixout_kernel,
        grid=(r // tm,),
        in_specs=[pl.BlockSpec((tm, d), lambda i: (i, 0)),
                  pl.BlockSpec((1, 3, d), lambda i: (i // bpm, 0, 0)),
                  pl.BlockSpec((tm, ret.shape[1]), lambda i: (i, 0)),
                  pl.BlockSpec((tm, mla.shape[1]), lambda i: (i, 0)),
                  _const_spec(w.shape)],
        out_specs=pl.BlockSpec((tm, d), lambda i: (i, 0)),
        out_shape=jax.ShapeDtypeStruct((r, d), F32),
        compiler_params=_cparams(("parallel",)),
        name="mixout",
    )(x, mod, ret, mla, w)


def _rope_tables(n):
    pos = np.arange(n, dtype=np.float64)
    inv_r = RET_ROPE_BASE ** (-np.arange(0, RET_DK, 2, dtype=np.float64) / RET_DK)
    ang = pos[:, None] * inv_r[None, :]
    c, s = np.cos(ang), np.sin(ang)
    rcos = np.tile(np.concatenate([c, c], -1), (1, LANES // RET_DK))
    rsin = np.tile(np.concatenate([-s, s], -1), (1, LANES // RET_DK))
    ax = MLA_ROPE // 2
    inv_a = AXIAL_BASE ** (-np.arange(0, ax, 2, dtype=np.float64) / ax)
    pr = (np.arange(n) // GRID_W).astype(np.float64)[:, None] * inv_a[None, :]
    pc = (np.arange(n) % GRID_W).astype(np.float64)[:, None] * inv_a[None, :]
    cr, sr, cc, sc = np.cos(pr), np.sin(pr), np.cos(pc), np.sin(pc)
    pad1 = np.ones((n, LANES - MLA_ROPE))
    pad0 = np.zeros((n, LANES - MLA_ROPE))
    acos = np.concatenate([cr, cr, cc, cc, pad1], -1)
    asin = np.concatenate([-sr, sr, -sc, sc, pad0], -1)
    return tuple(jnp.asarray(t.astype(np.float32)) for t in (rcos, rsin, acos, asin))


def kernel(x, c, ctx, c_ctx, ada_w, ada_b, norm1_g, ffn1_w_in, ffn1_w_out, norm2_g, mix_w_in,
           ret_decay_fwd, ret_decay_bwd, mla_q_norm_g, mla_w_uq, mla_kv_norm_g, mla_w_ukv,
           mix_w_out, norm3_g, ffn2_w_in, ffn2_w_out, final_norm_g):
    b, n, d = x.shape
    nctx = ctx.shape[1]
    depth = ada_w.shape[0]
    heads = ret_decay_fwd.shape[1]
    assert depth == 1, "single-layer block"
    assert mla_w_uq.shape[2] == heads * (MLA_NOPE + MLA_ROPE)
    l = 0

    cvec = jnp.concatenate([c, c_ctx[None, :]], 0)
    mods = _adaln(cvec, ada_w[l], ada_b[l]).reshape(b + 1, 9, d)
    m_lat, m_ctx = mods[:b], mods[b:b + 1]

    w1i, w1o = _ffn_weights(ffn1_w_in[l], ffn1_w_out[l])
    w2i, w2o = _ffn_weights(ffn2_w_in[l], ffn2_w_out[l])
    wmo = mix_w_out[l].astype(BF16)
    wmi = mix_w_in[l].astype(BF16)
    wuq = mla_w_uq[l].reshape(MLA_Q_RANK, heads, MLA_NOPE + MLA_ROPE)
    wuq = jnp.pad(wuq, ((0, 0), (0, 0), (0, MLA_QK_PAD - MLA_NOPE - MLA_ROPE)))
    wuq = wuq.reshape(MLA_Q_RANK, heads * MLA_QK_PAD).astype(BF16)
    wukv = mla_w_ukv[l].astype(BF16)

    x2 = x.reshape(b * n, d)
    c2 = ctx.reshape(b * nctx, d)

    x2 = _ffn(x2, m_lat[:, 0:3], norm1_g[l], w1i, w1o, rows_per_mod=n)
    c2 = _ffn(c2, m_ctx[:, 0:3], norm1_g[l], w1i, w1o, rows_per_mod=b * nctx)

    tables = _rope_tables(n)
    rq, rk, rv, rg, q, k, v = _proj(x2, m_lat[:, 3:6], norm2_g[l], wmi, mla_q_norm_g[l], wuq,
                                    mla_kv_norm_g[l], wukv, heads, b, tables)
    crk, crv, ck, cv = _proj(c2, m_ctx[:, 3:6], norm2_g[l], wmi, mla_q_norm_g[l], wuq,
                             mla_kv_norm_g[l], wukv, heads, b)
    dec = jnp.stack([ret_decay_fwd[l], ret_decay_bwd[l]]).astype(F32)
    ret = _retention(dec, rq, rk, rv, rg, crk, crv)
    mla = _mla(q, k, v, ck, cv)
    x2 = _mixout(x2, m_lat[:, 3:6], ret.reshape(b * n, -1), mla.reshape(b * n, -1), wmo,
                 rows_per_mod=n)

    x2 = _ffn(x2, m_lat[:, 6:9], norm3_g[l], w2i, w2o, rows_per_mod=n, final_g=final_norm_g)
    return x2.reshape(b, n, d)
```

```python
import functools
import math

import numpy as np

import jax
import jax.numpy as jnp
from jax import lax
from jax.experimental import pallas as pl
from jax.experimental.pallas import tpu as pltpu

GRID_W = 64
RET_DK = 64
RET_DV = 128
RET_ROPE_BASE = 10000.0
MLA_Q_RANK = 512
MLA_KV_RANK = 256
MLA_NOPE = 128
MLA_ROPE = 64
MLA_V = 128
AXIAL_BASE = 10000.0
RMS_EPS = 1e-6
GN_EPS = 1e-5

LANES = 128
MLA_QK_PAD = 2 * LANES
VMEM_LIMIT = 56 * 1024 * 1024

F32 = jnp.float32
BF16 = jnp.bfloat16


def _cparams(sem):
    return pltpu.CompilerParams(dimension_semantics=sem, vmem_limit_bytes=VMEM_LIMIT)


def _const_spec(shape):
    nd = len(shape)
    return pl.BlockSpec(shape, lambda *_: (0,) * nd, pipeline_mode=pl.Buffered(1))


def _silu(x):
    return x / (1.0 + jnp.exp(-x))


def _rms(x, g):
    return x * lax.rsqrt(jnp.mean(x * x, axis=-1, keepdims=True) + RMS_EPS) * g


def _ada_kernel(ct_ref, w_ref, b_ref, o_ref, sb_ref):
    m = sb_ref.shape[0]

    @pl.when(pl.program_id(0) == 0)
    def _():
        s = _silu(ct_ref[...])
        for r in range(m):
            sb_ref[r] = jnp.broadcast_to(s[:, r:r + 1], sb_ref.shape[1:])

    sub = 8
    col_tiles = [slice(t * LANES, (t + 1) * LANES) for t in range(w_ref.shape[1] // LANES)]

    def body(kc, accs):
        rows = pl.ds(pl.multiple_of(kc * sub, sub), sub)
        s_rows = [sb_ref[r, rows, :] for r in range(m)]
        out = []
        for t, cols in enumerate(col_tiles):
            wv = w_ref[rows, cols]
            out += [accs[t * m + r] + s_rows[r] * wv for r in range(m)]
        return tuple(out)

    zero = jnp.zeros((sub, LANES), F32)
    accs = lax.fori_loop(0, w_ref.shape[0] // sub, body, (zero,) * (m * len(col_tiles)),
                         unroll=4)
    for t, cols in enumerate(col_tiles):
        for r in range(m):
            o_ref[r:r + 1, cols] = jnp.sum(accs[t * m + r], axis=0, keepdims=True) + b_ref[:, cols]


def _adaln(cvec, ada_w, ada_b, tn_max=1024):
    m, d = cvec.shape
    n = ada_w.shape[1]
    tn = max(t for t in range(LANES, tn_max + 1, LANES) if n % t == 0)
    return pl.pallas_call(
        _ada_kernel,
        grid=(n // tn,),
        in_specs=[pl.BlockSpec((d, m), lambda j: (0, 0)),
                  pl.BlockSpec((d, tn), lambda j: (0, j)),
                  pl.BlockSpec((1, tn), lambda j: (0, j))],
        out_specs=pl.BlockSpec((m, tn), lambda j: (0, j)),
        out_shape=jax.ShapeDtypeStruct((m, n), F32),
        scratch_shapes=[pltpu.VMEM((m, d, LANES), F32)],
        compiler_params=_cparams(("arbitrary",)),
        name="adaln",
    )(cvec.T, ada_w, ada_b.reshape(1, n))


def _ffn_kernel(x_ref, mod_ref, g_ref, wi_hbm, wo_hbm, *rest, final):
    if final:
        fg_ref, o_ref, xn_ref, act_ref, wi_buf, wo_buf, sem = rest
    else:
        o_ref, xn_ref, act_ref, wi_buf, wo_buf, sem = rest
    i = pl.program_id(0)
    n_blocks = pl.num_programs(0)
    nf, tf = wo_hbm.shape[0], wo_hbm.shape[1]
    p_last = (nf - 1) % 2

    def wi_copy(c, slot):
        return pltpu.make_async_copy(wi_hbm.at[c], wi_buf.at[slot], sem.at[0, slot])

    def wo_copy(c, slot):
        return pltpu.make_async_copy(wo_hbm.at[c], wo_buf.at[slot], sem.at[1, slot])

    def down(c, p):
        wo_copy(c, p).wait()
        o_ref[...] += jnp.dot(act_ref[p], wo_buf[p], preferred_element_type=F32)

    def step(c, p, first=False, last=False):
        if not last:
            wi_copy(c + 1, 1 - p).start()
        wo_copy(c, p).start()
        wi_copy(c, p).wait()
        h = jnp.dot(xn_ref[...], wi_buf[p], preferred_element_type=F32)
        act_ref[p] = (_silu(h[:, :tf]) * h[:, tf:]).astype(BF16)
        if not first:
            down(c - 1, 1 - p)

    @pl.when(i == 0)
    def _():
        wi_copy(0, 0).start()

    y = _rms(x_ref[...], g_ref[...])
    xn_ref[...] = (y * (1.0 + mod_ref[0, 1:2, :]) + mod_ref[0, 0:1, :]).astype(BF16)
    o_ref[...] = jnp.zeros_like(o_ref)

    step(0, 0, first=True)
    n_mid = nf - 2

    def pair(t, carry):
        step(2 * t + 1, 1)
        step(2 * t + 2, 0)
        return carry

    lax.fori_loop(0, n_mid // 2, pair, 0)
    if n_mid % 2:
        step(nf - 2, 1 - p_last)
    step(nf - 1, p_last, last=True)
    down(nf - 1, p_last)
    wi_copy(0, 0).start()

    out = x_ref[...] + (0.5 * mod_ref[0, 2:3, :]) * o_ref[...]
    if final:
        out = _rms(out, fg_ref[...])
    o_ref[...] = out

    @pl.when(i == n_blocks - 1)
    def _():
        wi_copy(0, 0).wait()


FFN_TF = 512


def _retile_kernel(g_ref, u_ref, o_ref):
    tf = g_ref.shape[1]
    o_ref[0, :, :tf] = g_ref[...].astype(BF16)
    o_ref[0, :, tf:] = u_ref[...].astype(BF16)


def _ffn_weights(w_in, w_out, tf=FFN_TF):
    d, f2 = w_in.shape
    nf = f2 // 2 // tf
    w = pl.pallas_call(
        _retile_kernel,
        grid=(nf,),
        in_specs=[pl.BlockSpec((d, tf), lambda j: (0, j)),
                  pl.BlockSpec((d, tf), lambda j: (0, j + nf))],
        out_specs=pl.BlockSpec((1, d, 2 * tf), lambda j: (j, 0, 0)),
        out_shape=jax.ShapeDtypeStruct((nf, d, 2 * tf), BF16),
        compiler_params=_cparams(("parallel",)),
        name="ffn_w_retile",
    )(w_in, w_in)
    return w, w_out.astype(BF16)


def _ffn(x, mod, g, w_in, w_out, rows_per_mod, final_g=None, tm=512, tf=FFN_TF):
    r, d = x.shape
    f = w_out.shape[0]
    tm = min(tm, r)
    nf = f // tf
    assert nf >= 2, "the chunk pipeline peels a first and a last chunk"
    bpm = rows_per_mod // tm
    final = final_g is not None
    in_specs = [
        pl.BlockSpec((tm, d), lambda i: (i, 0)),
        pl.BlockSpec((1, 3, d), lambda i: (i // bpm, 0, 0)),
        pl.BlockSpec((1, d), lambda i: (0, 0)),
        pl.BlockSpec(memory_space=pl.ANY),
        pl.BlockSpec(memory_space=pl.ANY),
    ]
    args = [x, mod, g.reshape(1, d), w_in, w_out.reshape(nf, tf, d)]
    if final:
        in_specs.append(pl.BlockSpec((1, d), lambda i: (0, 0)))
        args.append(final_g.reshape(1, d))
    return pl.pallas_call(
        functools.partial(_ffn_kernel, final=final),
        grid=(r // tm,),
        in_specs=in_specs,
        out_specs=pl.BlockSpec((tm, d), lambda i: (i, 0)),
        out_shape=jax.ShapeDtypeStruct((r, d), F32),
        scratch_shapes=[pltpu.VMEM((tm, d), BF16), pltpu.VMEM((2, tm, tf), BF16),
                        pltpu.VMEM((2, d, 2 * tf), BF16), pltpu.VMEM((2, tf, d), BF16),
                        pltpu.SemaphoreType.DMA((2, 2))],
        compiler_params=_cparams(("arbitrary",)),
        name="ffn_final" if final else "ffn",
    )(*args)


def _swap_halves(x, half):
    lane = lax.broadcasted_iota(jnp.int32, x.shape, 1)
    first = (lane % (2 * half)) < half
    return jnp.where(first, pltpu.roll(x, LANES - half, 1), pltpu.roll(x, half, 1))


def _rope(x, cos, sin, half):
    return x * cos + _swap_halves(x, half) * sin


def _proj_kernel(x_ref, mod_ref, g_ref, w_ref, qg_ref, wuq_ref, kvg_ref, wukv_ref, *rest,
                 heads, latent):
    if latent:
        (rcos_ref, rsin_ref, acos_ref, asin_ref,
         rq_ref, rk_ref, rv_ref, rg_ref, q_ref, k_ref, v_ref) = rest
    else:
        rk_ref, rv_ref, k_ref, v_ref = rest
    hk = heads * RET_DK
    hv = heads * RET_DV
    o_rk, o_rv, o_rg = hk, 2 * hk, 2 * hk + hv
    o_cq = o_rg + hv
    o_ckv = o_cq + MLA_Q_RANK
    o_kr = o_ckv + MLA_KV_RANK

    y = _rms(x_ref[...], g_ref[...])
    h = (y * (1.0 + mod_ref[0, 1:2, :]) + mod_ref[0, 0:1, :]).astype(BF16)
    p = jnp.dot(h, w_ref[...], preferred_element_type=F32)

    for gidx in range(hk // LANES):
        sl = slice(gidx * LANES, (gidx + 1) * LANES)
        kk = p[:, o_rk + gidx * LANES:o_rk + (gidx + 1) * LANES] * (RET_DK ** -0.5)
        if latent:
            kk = _rope(kk, rcos_ref[...], rsin_ref[...], RET_DK // 2)
            qq = _rope(p[:, sl], rcos_ref[...], rsin_ref[...], RET_DK // 2)
        for t in range(LANES // RET_DK):
            hd = gidx * (LANES // RET_DK) + t
            rk_ref[0, hd] = kk[:, t * RET_DK:(t + 1) * RET_DK].astype(BF16)
            if latent:
                rq_ref[0, hd] = qq[:, t * RET_DK:(t + 1) * RET_DK].astype(BF16)
    for hd in range(heads):
        rv_ref[0, hd] = p[:, o_rv + hd * RET_DV:o_rv + (hd + 1) * RET_DV].astype(BF16)
    if latent:
        rg_ref[0] = p[:, o_rg:o_rg + hv].astype(BF16)

    kvn = _rms(p[:, o_ckv:o_ckv + MLA_KV_RANK], kvg_ref[...]).astype(BF16)
    kv = jnp.dot(kvn, wukv_ref[...], preferred_element_type=F32)
    kr = p[:, o_kr:o_kr + MLA_ROPE]
    kr = jnp.concatenate([kr, jnp.zeros((kr.shape[0], LANES - MLA_ROPE), F32)], axis=-1)
    if latent:
        kr = _rope(kr, acos_ref[...], asin_ref[...], MLA_ROPE // 4)
    kr = kr.astype(BF16)
    for hd in range(heads):
        base = hd * (MLA_NOPE + MLA_V)
        k_ref[0, hd, :, 0:MLA_NOPE] = kv[:, base:base + MLA_NOPE].astype(BF16)
        k_ref[0, hd, :, MLA_NOPE:MLA_QK_PAD] = kr
        v_ref[0, hd] = kv[:, base + MLA_NOPE:base + MLA_NOPE + MLA_V].astype(BF16)
    if latent:
        qn = _rms(p[:, o_cq:o_cq + MLA_Q_RANK], qg_ref[...]).astype(BF16)
        q = jnp.dot(qn, wuq_ref[...], preferred_element_type=F32)
        scale = (MLA_NOPE + MLA_ROPE) ** -0.5 * math.log2(math.e)
        for hd in range(heads):
            base = hd * MLA_QK_PAD
            q_ref[0, hd, :, 0:MLA_NOPE] = (q[:, base:base + MLA_NOPE] * scale).astype(BF16)
            qr = _rope(q[:, base + MLA_NOPE:base + MLA_QK_PAD],
                       acos_ref[...], asin_ref[...], MLA_ROPE // 4)
            q_ref[0, hd, :, MLA_NOPE:MLA_QK_PAD] = (qr * scale).astype(BF16)


def _proj(x, mod, g, w_in_p, qg, wuq_p, kvg, wukv, heads, batch, tables=None, tm=256):
    r, d = x.shape
    n = r // batch
    tm = min(tm, n)
    nb = n // tm
    latent = tables is not None
    row = lambda i: (i, 0)
    cst2 = lambda i: (0, 0)
    in_specs = [
        pl.BlockSpec((tm, d), row),
        pl.BlockSpec((1, 3, d), lambda i: (i // nb if latent else 0, 0, 0)),
        pl.BlockSpec((1, d), cst2),
        _const_spec(w_in_p.shape),
        pl.BlockSpec((1, MLA_Q_RANK), cst2),
        _const_spec(wuq_p.shape),
        pl.BlockSpec((1, MLA_KV_RANK), cst2),
        _const_spec(wukv.shape),
    ]
    args = [x, mod, g.reshape(1, d), w_in_p, qg.reshape(1, -1), wuq_p, kvg.reshape(1, -1), wukv]
    hmaj = lambda w: pl.BlockSpec((1, heads, tm, w), lambda i: (i // nb, 0, i % nb, 0))
    hshape = lambda w: jax.ShapeDtypeStruct((batch, heads, n, w), BF16)
    if latent:
        in_specs += [pl.BlockSpec((tm, LANES), lambda i: (i % nb, 0))] * 4
        args += list(tables)
        out_specs = [hmaj(RET_DK), hmaj(RET_DK), hmaj(RET_DV),
                     pl.BlockSpec((1, tm, heads * RET_DV), lambda i: (i // nb, i % nb, 0)),
                     hmaj(MLA_QK_PAD), hmaj(MLA_QK_PAD), hmaj(MLA_V)]
        out_shape = [hshape(RET_DK), hshape(RET_DK), hshape(RET_DV),
                     jax.ShapeDtypeStruct((batch, n, heads * RET_DV), BF16),
                     hshape(MLA_QK_PAD), hshape(MLA_QK_PAD), hshape(MLA_V)]
    else:
        out_specs = [hmaj(RET_DK), hmaj(RET_DV), hmaj(MLA_QK_PAD), hmaj(MLA_V)]
        out_shape = [hshape(RET_DK), hshape(RET_DV), hshape(MLA_QK_PAD), hshape(MLA_V)]
    return pl.pallas_call(
        functools.partial(_proj_kernel, heads=heads, latent=latent),
        grid=(r // tm,),
        in_specs=in_specs,
        out_specs=out_specs,
        out_shape=out_shape,
        compiler_params=_cparams(("parallel",)),
        name="proj_lat" if latent else "proj_ctx",
    )(*args)


def _log_sigmoid(x):
    return jnp.minimum(x, 0.0) - jnp.log1p(jnp.exp(-jnp.abs(x)))


def _ret_kernel(dec_ref, q_ref, k_ref, v_ref, g_ref, ck_ref, cv_ref, o_ref, kv_ref, st_ref,
                *, chunk, unroll):
    hd = pl.program_id(1)
    n = q_ref.shape[2]
    nctx = ck_ref.shape[2]
    nc = n // chunk
    dk = RET_DK
    lgf = _log_sigmoid(jnp.full((1, 1), dec_ref[0, hd], F32))
    lgb = _log_sigmoid(jnp.full((1, 1), dec_ref[1, hd], F32))

    def col(m):
        return lax.broadcasted_iota(jnp.int32, (m, 1), 0).astype(F32)

    def weighted_kv(k, v, zf, zb):
        kf = k.astype(F32)
        kz = jnp.concatenate([kf * zf, kf * zb], axis=-1).astype(BF16)
        return lax.dot_general(kz, v, (((0,), (0,)), ((), ())), preferred_element_type=F32)

    ic = col(nctx)
    s0 = weighted_kv(ck_ref[0, 0], cv_ref[0, 0],
                     jnp.exp(lgf * (nctx - 1.0 - ic)), jnp.exp(lgb * ic))

    i = col(chunk)
    zeta_f = jnp.exp(lgf * (chunk - 1.0 - i))
    zeta_b = jnp.exp(lgb * i)
    xi_f = jnp.exp(lgf * (i + 1.0))
    xi_b = jnp.exp(lgb * (chunk - i))
    dec_f = jnp.exp(lgf * float(chunk))
    dec_b = jnp.exp(lgb * float(chunk))
    diff = i - lax.broadcasted_iota(jnp.int32, (1, chunk), 1).astype(F32)
    dmat = jnp.where(diff >= 0.0, jnp.exp(lgf * jnp.maximum(diff, 0.0)),
                     jnp.exp(lgb * jnp.maximum(-diff, 0.0)))

    def rows(c):
        return pl.ds(pl.multiple_of(c * chunk, chunk), chunk)

    def p1(c, carry):
        kv_ref[c] = weighted_kv(k_ref[0, 0, rows(c), :], v_ref[0, 0, rows(c), :], zeta_f, zeta_b)
        return carry
    lax.fori_loop(0, nc, p1, 0, unroll=unroll)

    dv = v_ref.shape[3]

    def p2f(c, s):
        st_ref[c, :, 0:dv] = s.astype(BF16)
        return dec_f * s + kv_ref[c, 0:dk, :]
    lax.fori_loop(0, nc, p2f, s0[0:dk])

    def p2b(t, s):
        c = nc - 1 - t
        st_ref[c, :, dv:2 * dv] = s.astype(BF16)
        return dec_b * s + kv_ref[c, dk:2 * dk, :]
    lax.fori_loop(0, nc, p2b, s0[dk:2 * dk])

    def p3(c, carry):
        q = q_ref[0, 0, rows(c), :]
        k = k_ref[0, 0, rows(c), :]
        v = v_ref[0, 0, rows(c), :]
        a = lax.dot_general(q, k, (((1,), (1,)), ((), ())), preferred_element_type=F32)
        y = jnp.dot((a * dmat).astype(BF16), v, preferred_element_type=F32)
        cross = jnp.dot(q, st_ref[c], preferred_element_type=F32)
        y += xi_f * cross[:, 0:dv] + xi_b * cross[:, dv:2 * dv]
        mu = jnp.mean(y, axis=-1, keepdims=True)
        yc = y - mu
        var = jnp.mean(yc * yc, axis=-1, keepdims=True)
        yn = yc * lax.rsqrt(var + GN_EPS)
        o_ref[0, rows(c), :] = (_silu(g_ref[0, rows(c), :].astype(F32)) * yn).astype(BF16)
        return carry
    lax.fori_loop(0, nc, p3, 0, unroll=unroll)


def _retention(dec, rq, rk, rv, rg, crk, crv, chunk=256, unroll=8):
    b, h, n, _ = rq.shape
    nctx = crk.shape[2]
    chunk = min(chunk, n)
    unroll = math.gcd(unroll, n // chunk)
    hm = lambda w, rows: pl.BlockSpec((1, 1, rows, w), lambda bi, hi: (bi, hi, 0, 0))
    return pl.pallas_call(
        functools.partial(_ret_kernel, chunk=chunk, unroll=unroll),
        grid=(b, h),
        in_specs=[pl.BlockSpec(memory_space=pltpu.SMEM),
                  hm(RET_DK, n), hm(RET_DK, n), hm(RET_DV, n),
                  pl.BlockSpec((1, n, RET_DV), lambda bi, hi: (bi, 0, hi)),
                  hm(RET_DK, nctx), hm(RET_DV, nctx)],
        out_specs=pl.BlockSpec((1, n, RET_DV), lambda bi, hi: (bi, 0, hi)),
        out_shape=jax.ShapeDtypeStruct((b, n, h * RET_DV), BF16),
        scratch_shapes=[pltpu.VMEM((n // chunk, 2 * RET_DK, RET_DV), F32),
                        pltpu.VMEM((n // chunk, RET_DK, 2 * RET_DV), BF16)],
        compiler_params=_cparams(("parallel", "parallel")),
        name="retention",
    )(dec, rq, rk, rv, rg, crk, crv)


def _mla_kernel(q_ref, k_ref, vp_ref, v_ref, ck_ref, cv_ref, o_ref,
                m_ref, acc_ref, s_ref, *, sub):
    j = pl.program_id(3)
    nj = pl.num_programs(3)
    nt = (((1,), (1,)), ((), ()))
    n_sub = q_ref.shape[2] // sub
    row_tiles = [slice(r * sub, (r + 1) * sub) for r in range(n_sub)]
    last = n_sub - 1

    def with_ones(v):
        return jnp.concatenate([v, jnp.ones_like(v)], axis=-1)

    def softmax_update(rows, s):
        m_old = m_ref[rows, :]
        m_new = jnp.maximum(m_old, jnp.max(s, axis=-1, keepdims=True))
        m_ref[rows, :] = m_new
        p = jnp.exp2(s - jnp.tile(m_new, (1, s.shape[1] // LANES))).astype(BF16)
        return jnp.exp2(m_old - m_new), p

    def accumulate(rows, alpha, p, v1):
        acc_ref[rows, :] = jnp.tile(alpha, (1, 2)) * acc_ref[rows, :] + jnp.dot(
            p, v1, preferred_element_type=F32)

    @pl.when(j == 0)
    def _():
        s = lax.dot_general(q_ref[0, 0], ck_ref[0, 0], nt, preferred_element_type=F32)
        m = jnp.max(s, axis=-1, keepdims=True)
        p = jnp.exp2(s - m)
        m_ref[...] = jnp.broadcast_to(m, m_ref.shape)
        acc_ref[...] = jnp.dot(p.astype(BF16), with_ones(cv_ref[0, 0]),
                               preferred_element_type=F32)
        s_ref[...] = jnp.full_like(s_ref, -jnp.inf)

    k = k_ref[0, 0]
    v_cur = with_ones(v_ref[0, 0])
    scores, alphas, probs = {}, {}, {}

    def stage_scores(r):
        scores[r] = lax.dot_general(q_ref[0, 0, row_tiles[r], :], k, nt,
                                    preferred_element_type=F32)

    def stage_softmax(r):
        alphas[r], probs[r] = softmax_update(row_tiles[r], scores.pop(r))

    def stage_values(r):
        accumulate(row_tiles[r], alphas.pop(r), probs.pop(r), v_cur)

    stage_scores(0)
    carried = softmax_update(row_tiles[last], s_ref[...])
    for t in range(1, n_sub + 1):
        if t < n_sub:
            stage_scores(t)
        if t - 1 < last:
            stage_softmax(t - 1)
        if t == 1:
            accumulate(row_tiles[last], *carried, with_ones(vp_ref[0, 0]))
        if 0 <= t - 2:
            stage_values(t - 2)
    s_ref[...] = scores.pop(last)

    @pl.when(j == nj - 1)
    def _():
        accumulate(row_tiles[last], *softmax_update(row_tiles[last], s_ref[...]), v_cur)
        acc = acc_ref[...]
        o_ref[0] = (acc[:, :MLA_V] / acc[:, MLA_V:]).astype(BF16)


def _mla(q, k, v, ck, cv, tq=4096, tk=2048, sub=512):
    b, h, n, w = q.shape
    nctx = ck.shape[2]
    tq = min(tq, n)
    tk = min(tk, n)
    sub = min(sub, tq)
    return pl.pallas_call(
        functools.partial(_mla_kernel, sub=sub),
        grid=(b, h, n // tq, n // tk),
        in_specs=[pl.BlockSpec((1, 1, tq, w), lambda bi, hi, i, j: (bi, hi, i, 0)),
                  pl.BlockSpec((1, 1, tk, w), lambda bi, hi, i, j: (bi, hi, j, 0)),
                  pl.BlockSpec((1, 1, tk, MLA_V),
                               lambda bi, hi, i, j: (bi, hi, jnp.maximum(j - 1, 0), 0)),
                  pl.BlockSpec((1, 1, tk, MLA_V), lambda bi, hi, i, j: (bi, hi, j, 0)),
                  pl.BlockSpec((1, 1, nctx, w), lambda bi, hi, i, j: (bi, hi, 0, 0)),
                  pl.BlockSpec((1, 1, nctx, MLA_V), lambda bi, hi, i, j: (bi, hi, 0, 0))],
        out_specs=pl.BlockSpec((1, tq, MLA_V), lambda bi, hi, i, j: (bi, i, hi)),
        out_shape=jax.ShapeDtypeStruct((b, n, h * MLA_V), BF16),
        scratch_shapes=[pltpu.VMEM((tq, LANES), F32), pltpu.VMEM((tq, 2 * MLA_V), F32),
                        pltpu.VMEM((sub, tk), F32)],
        compiler_params=_cparams(("parallel", "parallel", "parallel", "arbitrary")),
        name="mla_attn",
    )(q, k, v, v, ck, cv)


def _mixout_kernel(x_ref, mod_ref, ret_ref, mla_ref, w_ref, o_ref):
    hr = ret_ref.shape[1]
    mix = jnp.dot(ret_ref[...], w_ref[0:hr, :], preferred_element_type=F32)
    mix += jnp.dot(mla_ref[...], w_ref[hr:, :], preferred_element_type=F32)
    o_ref[...] = x_ref[...] + mod_ref[0, 2:3, :] * mix


def _mixout(x, mod, ret, mla, w, rows_per_mod, tm=512):
    r, d = x.shape
    tm = min(tm, r)
    bpm = rows_per_mod // tm
    return pl.pallas_call(
        _mixout_kernel,
        grid=(r // tm,),
        in_specs=[pl.BlockSpec((tm, d), lambda i: (i, 0)),
                  pl.BlockSpec((1, 3, d), lambda i: (i // bpm, 0, 0)),
                  pl.BlockSpec((tm, ret.shape[1]), lambda i: (i, 0)),
                  pl.BlockSpec((tm, mla.shape[1]), lambda i: (i, 0)),
                  _const_spec(w.shape)],
        out_specs=pl.BlockSpec((tm, d), lambda i: (i, 0)),
        out_shape=jax.ShapeDtypeStruct((r, d), F32),
        compiler_params=_cparams(("parallel",)),
        name="mixout",
    )(x, mod, ret, mla, w)


def _rope_tables(n):
    pos = np.arange(n, dtype=np.float64)
    inv_r = RET_ROPE_BASE ** (-np.arange(0, RET_DK, 2, dtype=np.float64) / RET_DK)
    ang = pos[:, None] * inv_r[None, :]
    c, s = np.cos(ang), np.sin(ang)
    rcos = np.tile(np.concatenate([c, c], -1), (1, LANES // RET_DK))
    rsin = np.tile(np.concatenate([-s, s], -1), (1, LANES // RET_DK))
    ax = MLA_ROPE // 2
    inv_a = AXIAL_BASE ** (-np.arange(0, ax, 2, dtype=np.float64) / ax)
    pr = (np.arange(n) // GRID_W).astype(np.float64)[:, None] * inv_a[None, :]
    pc = (np.arange(n) % GRID_W).astype(np.float64)[:, None] * inv_a[None, :]
    cr, sr, cc, sc = np.cos(pr), np.sin(pr), np.cos(pc), np.sin(pc)
    pad1 = np.ones((n, LANES - MLA_ROPE))
    pad0 = np.zeros((n, LANES - MLA_ROPE))
    acos = np.concatenate([cr, cr, cc, cc, pad1], -1)
    asin = np.concatenate([-sr, sr, -sc, sc, pad0], -1)
    return tuple(jnp.asarray(t.astype(np.float32)) for t in (rcos, rsin, acos, asin))


def kernel(x, c, ctx, c_ctx, ada_w, ada_b, norm1_g, ffn1_w_in, ffn1_w_out, norm2_g, mix_w_in,
           ret_decay_fwd, ret_decay_bwd, mla_q_norm_g, mla_w_uq, mla_kv_norm_g, mla_w_ukv,
           mix_w_out, norm3_g, ffn2_w_in, ffn2_w_out, final_norm_g):
    b, n, d = x.shape
    nctx = ctx.shape[1]
    depth = ada_w.shape[0]
    heads = ret_decay_fwd.shape[1]
    assert depth == 1, "single-layer block"
    assert mla_w_uq.shape[2] == heads * (MLA_NOPE + MLA_ROPE)
    l = 0

    cvec = jnp.concatenate([c, c_ctx[None, :]], 0)
    mods = _adaln(cvec, ada_w[l], ada_b[l]).reshape(b + 1, 9, d)
    m_lat, m_ctx = mods[:b], mods[b:b + 1]

    w1i, w1o = _ffn_weights(ffn1_w_in[l], ffn1_w_out[l])
    w2i, w2o = _ffn_weights(ffn2_w_in[l], ffn2_w_out[l])
    wmo = mix_w_out[l].astype(BF16)
    wmi = mix_w_in[l].astype(BF16)
    wuq = mla_w_uq[l].reshape(MLA_Q_RANK, heads, MLA_NOPE + MLA_ROPE)
    wuq = jnp.pad(wuq, ((0, 0), (0, 0), (0, MLA_QK_PAD - MLA_NOPE - MLA_ROPE)))
    wuq = wuq.reshape(MLA_Q_RANK, heads * MLA_QK_PAD).astype(BF16)
    wukv = mla_w_ukv[l].astype(BF16)

    x2 = x.reshape(b * n, d)
    c2 = ctx.reshape(b * nctx, d)

    x2 = _ffn(x2, m_lat[:, 0:3], norm1_g[l], w1i, w1o, rows_per_mod=n)
    c2 = _ffn(c2, m_ctx[:, 0:3], norm1_g[l], w1i, w1o, rows_per_mod=b * nctx)

    tables = _rope_tables(n)
    rq, rk, rv, rg, q, k, v = _proj(x2, m_lat[:, 3:6], norm2_g[l], wmi, mla_q_norm_g[l], wuq,
                                    mla_kv_norm_g[l], wukv, heads, b, tables)
    crk, crv, ck, cv = _proj(c2, m_ctx[:, 3:6], norm2_g[l], wmi, mla_q_norm_g[l], wuq,
                             mla_kv_norm_g[l], wukv, heads, b)
    dec = jnp.stack([ret_decay_fwd[l], ret_decay_bwd[l]]).astype(F32)
    ret = _retention(dec, rq, rk, rv, rg, crk, crv)
    mla = _mla(q, k, v, ck, cv)
    x2 = _mixout(x2, m_lat[:, 3:6], ret.reshape(b * n, -1), mla.reshape(b * n, -1), wmo,
                 rows_per_mod=n)

    x2 = _ffn(x2, m_lat[:, 6:9], norm3_g[l], w2i, w2o, rows_per_mod=n, final_g=final_norm_g)
    return x2.reshape(b, n, d)
```

```python
import functools
import math

import numpy as np

import jax
import jax.numpy as jnp
from jax import lax
from jax.experimental import pallas as pl
from jax.experimental.pallas import tpu as pltpu

GRID_W = 64
RET_DK = 64
RET_DV = 128
RET_ROPE_BASE = 10000.0
MLA_Q_RANK = 512
MLA_KV_RANK = 256
MLA_NOPE = 128
MLA_ROPE = 64
MLA_V = 128
AXIAL_BASE = 10000.0
RMS_EPS = 1e-6
GN_EPS = 1e-5

LANES = 128
MLA_QK_PAD = 2 * LANES
VMEM_LIMIT = 56 * 1024 * 1024

F32 = jnp.float32
BF16 = jnp.bfloat16


def _cparams(sem):
    return pltpu.CompilerParams(dimension_semantics=sem, vmem_limit_bytes=VMEM_LIMIT)


def _const_spec(shape):
    nd = len(shape)
    return pl.BlockSpec(shape, lambda *_: (0,) * nd, pipeline_mode=pl.Buffered(1))


def _silu(x):
    return x / (1.0 + jnp.exp(-x))


def _rms(x, g):
    return x * lax.rsqrt(jnp.mean(x * x, axis=-1, keepdims=True) + RMS_EPS) * g


def _ada_kernel(ct_ref, w_ref, b_ref, o_ref, sb_ref):
    m = sb_ref.shape[0]

    @pl.when(pl.program_id(0) == 0)
    def _():
        s = _silu(ct_ref[...])
        for r in range(m):
            sb_ref[r] = jnp.broadcast_to(s[:, r:r + 1], sb_ref.shape[1:])

    sub = 8
    col_tiles = [slice(t * LANES, (t + 1) * LANES) for t in range(w_ref.shape[1] // LANES)]

    def body(kc, accs):
        rows = pl.ds(pl.multiple_of(kc * sub, sub), sub)
        s_rows = [sb_ref[r, rows, :] for r in range(m)]
        out = []
        for t, cols in enumerate(col_tiles):
            wv = w_ref[rows, cols]
            out += [accs[t * m + r] + s_rows[r] * wv for r in range(m)]
        return tuple(out)

    zero = jnp.zeros((sub, LANES), F32)
    accs = lax.fori_loop(0, w_ref.shape[0] // sub, body, (zero,) * (m * len(col_tiles)),
                         unroll=4)
    for t, cols in enumerate(col_tiles):
        for r in range(m):
            o_ref[r:r + 1, cols] = jnp.sum(accs[t * m + r], axis=0, keepdims=True) + b_ref[:, cols]


def _adaln(cvec, ada_w, ada_b, tn_max=1024):
    m, d = cvec.shape
    n = ada_w.shape[1]
    tn = max(t for t in range(LANES, tn_max + 1, LANES) if n % t == 0)
    return pl.pallas_call(
        _ada_kernel,
        grid=(n // tn,),
        in_specs=[pl.BlockSpec((d, m), lambda j: (0, 0)),
                  pl.BlockSpec((d, tn), lambda j: (0, j)),
                  pl.BlockSpec((1, tn), lambda j: (0, j))],
        out_specs=pl.BlockSpec((m, tn), lambda j: (0, j)),
        out_shape=jax.ShapeDtypeStruct((m, n), F32),
        scratch_shapes=[pltpu.VMEM((m, d, LANES), F32)],
        compiler_params=_cparams(("arbitrary",)),
        name="adaln",
    )(cvec.T, ada_w, ada_b.reshape(1, n))


def _ffn_kernel(x_ref, mod_ref, g_ref, wi_hbm, wo_hbm, *rest, final):
    if final:
        fg_ref, o_ref, xn_ref, wi_buf, wo_buf, sem = rest
    else:
        o_ref, xn_ref, wi_buf, wo_buf, sem = rest
    i = pl.program_id(0)
    n_blocks = pl.num_programs(0)
    nf, tf = wo_hbm.shape[0], wo_hbm.shape[1]

    def copies(c, slot):
        return (pltpu.make_async_copy(wi_hbm.at[c], wi_buf.at[slot], sem.at[0, slot]),
                pltpu.make_async_copy(wo_hbm.at[c], wo_buf.at[slot], sem.at[1, slot]))

    @pl.when(i == 0)
    def _():
        for cp in copies(0, 0):
            cp.start()

    y = _rms(x_ref[...], g_ref[...])
    xn_ref[...] = (y * (1.0 + mod_ref[0, 1:2, :]) + mod_ref[0, 0:1, :]).astype(BF16)
    o_ref[...] = jnp.zeros_like(o_ref)

    def chunk(c, carry):
        step = i * nf + c
        slot = step % 2

        @pl.when(step + 1 < n_blocks * nf)
        def _():
            for cp in copies(jnp.where(c + 1 == nf, 0, c + 1), 1 - slot):
                cp.start()

        for cp in copies(c, slot):
            cp.wait()
        h = jnp.dot(xn_ref[...], wi_buf[slot], preferred_element_type=F32)
        act = (_silu(h[:, :tf]) * h[:, tf:]).astype(BF16)
        o_ref[...] += jnp.dot(act, wo_buf[slot], preferred_element_type=F32)
        return carry

    lax.fori_loop(0, nf, chunk, 0)

    out = x_ref[...] + (0.5 * mod_ref[0, 2:3, :]) * o_ref[...]
    if final:
        out = _rms(out, fg_ref[...])
    o_ref[...] = out


FFN_TF = 512


def _retile_kernel(g_ref, u_ref, o_ref):
    tf = g_ref.shape[1]
    o_ref[0, :, :tf] = g_ref[...].astype(BF16)
    o_ref[0, :, tf:] = u_ref[...].astype(BF16)


def _ffn_weights(w_in, w_out, tf=FFN_TF):
    d, f2 = w_in.shape
    nf = f2 // 2 // tf
    w = pl.pallas_call(
        _retile_kernel,
        grid=(nf,),
        in_specs=[pl.BlockSpec((d, tf), lambda j: (0, j)),
                  pl.BlockSpec((d, tf), lambda j: (0, j + nf))],
        out_specs=pl.BlockSpec((1, d, 2 * tf), lambda j: (j, 0, 0)),
        out_shape=jax.ShapeDtypeStruct((nf, d, 2 * tf), BF16),
        compiler_params=_cparams(("parallel",)),
        name="ffn_w_retile",
    )(w_in, w_in)
    return w, w_out.astype(BF16)


def _ffn(x, mod, g, w_in, w_out, rows_per_mod, final_g=None, tm=512, tf=FFN_TF):
    r, d = x.shape
    f = w_out.shape[0]
    tm = min(tm, r)
    nf = f // tf
    bpm = rows_per_mod // tm
    final = final_g is not None
    in_specs = [
        pl.BlockSpec((tm, d), lambda i: (i, 0)),
        pl.BlockSpec((1, 3, d), lambda i: (i // bpm, 0, 0)),
        pl.BlockSpec((1, d), lambda i: (0, 0)),
        pl.BlockSpec(memory_space=pl.ANY),
        pl.BlockSpec(memory_space=pl.ANY),
    ]
    args = [x, mod, g.reshape(1, d), w_in, w_out.reshape(nf, tf, d)]
    if final:
        in_specs.append(pl.BlockSpec((1, d), lambda i: (0, 0)))
        args.append(final_g.reshape(1, d))
    return pl.pallas_call(
        functools.partial(_ffn_kernel, final=final),
        grid=(r // tm,),
        in_specs=in_specs,
        out_specs=pl.BlockSpec((tm, d), lambda i: (i, 0)),
        out_shape=jax.ShapeDtypeStruct((r, d), F32),
        scratch_shapes=[pltpu.VMEM((tm, d), BF16),
                        pltpu.VMEM((2, d, 2 * tf), BF16), pltpu.VMEM((2, tf, d), BF16),
                        pltpu.SemaphoreType.DMA((2, 2))],
        compiler_params=_cparams(("arbitrary",)),
        name="ffn_final" if final else "ffn",
    )(*args)


def _swap_halves(x, half):
    lane = lax.broadcasted_iota(jnp.int32, x.shape, 1)
    first = (lane % (2 * half)) < half
    return jnp.where(first, pltpu.roll(x, LANES - half, 1), pltpu.roll(x, half, 1))


def _rope(x, cos, sin, half):
    return x * cos + _swap_halves(x, half) * sin


def _proj_kernel(x_ref, mod_ref, g_ref, w_ref, qg_ref, wuq_ref, kvg_ref, wukv_ref, *rest,
                 heads, latent):
    if latent:
        (rcos_ref, rsin_ref, acos_ref, asin_ref,
         rq_ref, rk_ref, rv_ref, rg_ref, q_ref, k_ref, v_ref) = rest
    else:
        rk_ref, rv_ref, k_ref, v_ref = rest
    hk = heads * RET_DK
    hv = heads * RET_DV
    o_rk, o_rv, o_rg = hk, 2 * hk, 2 * hk + hv
    o_cq = o_rg + hv
    o_ckv = o_cq + MLA_Q_RANK
    o_kr = o_ckv + MLA_KV_RANK

    y = _rms(x_ref[...], g_ref[...])
    h = (y * (1.0 + mod_ref[0, 1:2, :]) + mod_ref[0, 0:1, :]).astype(BF16)
    p = jnp.dot(h, w_ref[...], preferred_element_type=F32)

    for gidx in range(hk // LANES):
        sl = slice(gidx * LANES, (gidx + 1) * LANES)
        kk = p[:, o_rk + gidx * LANES:o_rk + (gidx + 1) * LANES] * (RET_DK ** -0.5)
        if latent:
            kk = _rope(kk, rcos_ref[...], rsin_ref[...], RET_DK // 2)
            qq = _rope(p[:, sl], rcos_ref[...], rsin_ref[...], RET_DK // 2)
        for t in range(LANES // RET_DK):
            hd = gidx * (LANES // RET_DK) + t
            rk_ref[0, hd] = kk[:, t * RET_DK:(t + 1) * RET_DK].astype(BF16)
            if latent:
                rq_ref[0, hd] = qq[:, t * RET_DK:(t + 1) * RET_DK].astype(BF16)
    for hd in range(heads):
        rv_ref[0, hd] = p[:, o_rv + hd * RET_DV:o_rv + (hd + 1) * RET_DV].astype(BF16)
    if latent:
        rg_ref[0] = p[:, o_rg:o_rg + hv].astype(BF16)

    kvn = _rms(p[:, o_ckv:o_ckv + MLA_KV_RANK], kvg_ref[...]).astype(BF16)
    kv = jnp.dot(kvn, wukv_ref[...], preferred_element_type=F32)
    kr = p[:, o_kr:o_kr + MLA_ROPE]
    kr = jnp.concatenate([kr, jnp.zeros((kr.shape[0], LANES - MLA_ROPE), F32)], axis=-1)
    if latent:
        kr = _rope(kr, acos_ref[...], asin_ref[...], MLA_ROPE // 4)
    kr = kr.astype(BF16)
    for hd in range(heads):
        base = hd * (MLA_NOPE + MLA_V)
        k_ref[0, hd, :, 0:MLA_NOPE] = kv[:, base:base + MLA_NOPE].astype(BF16)
        k_ref[0, hd, :, MLA_NOPE:MLA_QK_PAD] = kr
        v_ref[0, hd] = kv[:, base + MLA_NOPE:base + MLA_NOPE + MLA_V].astype(BF16)
    if latent:
        qn = _rms(p[:, o_cq:o_cq + MLA_Q_RANK], qg_ref[...]).astype(BF16)
        q = jnp.dot(qn, wuq_ref[...], preferred_element_type=F32)
        scale = (MLA_NOPE + MLA_ROPE) ** -0.5 * math.log2(math.e)
        for hd in range(heads):
            base = hd * MLA_QK_PAD
            q_ref[0, hd, :, 0:MLA_NOPE] = (q[:, base:base + MLA_NOPE] * scale).astype(BF16)
            qr = _rope(q[:, base + MLA_NOPE:base + MLA_QK_PAD],
                       acos_ref[...], asin_ref[...], MLA_ROPE // 4)
            q_ref[0, hd, :, MLA_NOPE:MLA_QK_PAD] = (qr * scale).astype(BF16)


def _proj(x, mod, g, w_in_p, qg, wuq_p, kvg, wukv, heads, batch, tables=None, tm=256):
    r, d = x.shape
    n = r // batch
    tm = min(tm, n)
    nb = n // tm
    latent = tables is not None
    row = lambda i: (i, 0)
    cst2 = lambda i: (0, 0)
    in_specs = [
        pl.BlockSpec((tm, d), row),
        pl.BlockSpec((1, 3, d), lambda i: (i // nb if latent else 0, 0, 0)),
        pl.BlockSpec((1, d), cst2),
        _const_spec(w_in_p.shape),
        pl.BlockSpec((1, MLA_Q_RANK), cst2),
        _const_spec(wuq_p.shape),
        pl.BlockSpec((1, MLA_KV_RANK), cst2),
        _const_spec(wukv.shape),
    ]
    args = [x, mod, g.reshape(1, d), w_in_p, qg.reshape(1, -1), wuq_p, kvg.reshape(1, -1), wukv]
    hmaj = lambda w: pl.BlockSpec((1, heads, tm, w), lambda i: (i // nb, 0, i % nb, 0))
    hshape = lambda w: jax.ShapeDtypeStruct((batch, heads, n, w), BF16)
    if latent:
        in_specs += [pl.BlockSpec((tm, LANES), lambda i: (i % nb, 0))] * 4
        args += list(tables)
        out_specs = [hmaj(RET_DK), hmaj(RET_DK), hmaj(RET_DV),
                     pl.BlockSpec((1, tm, heads * RET_DV), lambda i: (i // nb, i % nb, 0)),
                     hmaj(MLA_QK_PAD), hmaj(MLA_QK_PAD), hmaj(MLA_V)]
        out_shape = [hshape(RET_DK), hshape(RET_DK), hshape(RET_DV),
                     jax.ShapeDtypeStruct((batch, n, heads * RET_DV), BF16),
                     hshape(MLA_QK_PAD), hshape(MLA_QK_PAD), hshape(MLA_V)]
    else:
        out_specs = [hmaj(RET_DK), hmaj(RET_DV), hmaj(MLA_QK_PAD), hmaj(MLA_V)]
        out_shape = [hshape(RET_DK), hshape(RET_DV), hshape(MLA_QK_PAD), hshape(MLA_V)]
    return pl.pallas_call(
        functools.partial(_proj_kernel, heads=heads, latent=latent),
        grid=(r // tm,),
        in_specs=in_specs,
        out_specs=out_specs,
        out_shape=out_shape,
        compiler_params=_cparams(("parallel",)),
        name="proj_lat" if latent else "proj_ctx",
    )(*args)


def _log_sigmoid(x):
    return jnp.minimum(x, 0.0) - jnp.log1p(jnp.exp(-jnp.abs(x)))


def _ret_kernel(dec_ref, q_ref, k_ref, v_ref, g_ref, ck_ref, cv_ref, o_ref, kv_ref, st_ref,
                *, chunk, unroll):
    hd = pl.program_id(1)
    n = q_ref.shape[2]
    nctx = ck_ref.shape[2]
    nc = n // chunk
    dk = RET_DK
    lgf = _log_sigmoid(jnp.full((1, 1), dec_ref[0, hd], F32))
    lgb = _log_sigmoid(jnp.full((1, 1), dec_ref[1, hd], F32))

    def col(m):
        return lax.broadcasted_iota(jnp.int32, (m, 1), 0).astype(F32)

    def weighted_kv(k, v, zf, zb):
        kf = k.astype(F32)
        kz = jnp.concatenate([kf * zf, kf * zb], axis=-1).astype(BF16)
        return lax.dot_general(kz, v, (((0,), (0,)), ((), ())), preferred_element_type=F32)

    ic = col(nctx)
    s0 = weighted_kv(ck_ref[0, 0], cv_ref[0, 0],
                     jnp.exp(lgf * (nctx - 1.0 - ic)), jnp.exp(lgb * ic))

    i = col(chunk)
    zeta_f = jnp.exp(lgf * (chunk - 1.0 - i))
    zeta_b = jnp.exp(lgb * i)
    xi_f = jnp.exp(lgf * (i + 1.0))
    xi_b = jnp.exp(lgb * (chunk - i))
    dec_f = jnp.exp(lgf * float(chunk))
    dec_b = jnp.exp(lgb * float(chunk))
    diff = i - lax.broadcasted_iota(jnp.int32, (1, chunk), 1).astype(F32)
    dmat = jnp.where(diff >= 0.0, jnp.exp(lgf * jnp.maximum(diff, 0.0)),
                     jnp.exp(lgb * jnp.maximum(-diff, 0.0)))

    def rows(c):
        return pl.ds(pl.multiple_of(c * chunk, chunk), chunk)

    def p1(c, carry):
        kv_ref[c] = weighted_kv(k_ref[0, 0, rows(c), :], v_ref[0, 0, rows(c), :], zeta_f, zeta_b)
        return carry
    lax.fori_loop(0, nc, p1, 0, unroll=unroll)

    dv = v_ref.shape[3]

    def p2f(c, s):
        st_ref[c, :, 0:dv] = s.astype(BF16)
        return dec_f * s + kv_ref[c, 0:dk, :]
    lax.fori_loop(0, nc, p2f, s0[0:dk])

    def p2b(t, s):
        c = nc - 1 - t
        st_ref[c, :, dv:2 * dv] = s.astype(BF16)
        return dec_b * s + kv_ref[c, dk:2 * dk, :]
    lax.fori_loop(0, nc, p2b, s0[dk:2 * dk])

    def p3(c, carry):
        q = q_ref[0, 0, rows(c), :]
        k = k_ref[0, 0, rows(c), :]
        v = v_ref[0, 0, rows(c), :]
        a = lax.dot_general(q, k, (((1,), (1,)), ((), ())), preferred_element_type=F32)
        y = jnp.dot((a * dmat).astype(BF16), v, preferred_element_type=F32)
        cross = jnp.dot(q, st_ref[c], preferred_element_type=F32)
        y += xi_f * cross[:, 0:dv] + xi_b * cross[:, dv:2 * dv]
        mu = jnp.mean(y, axis=-1, keepdims=True)
        yc = y - mu
        var = jnp.mean(yc * yc, axis=-1, keepdims=True)
        yn = yc * lax.rsqrt(var + GN_EPS)
        o_ref[0, rows(c), :] = (_silu(g_ref[0, rows(c), :].astype(F32)) * yn).astype(BF16)
        return carry
    lax.fori_loop(0, nc, p3, 0, unroll=unroll)


def _retention(dec, rq, rk, rv, rg, crk, crv, chunk=256, unroll=8):
    b, h, n, _ = rq.shape
    nctx = crk.shape[2]
    chunk = min(chunk, n)
    unroll = math.gcd(unroll, n // chunk)
    hm = lambda w, rows: pl.BlockSpec((1, 1, rows, w), lambda bi, hi: (bi, hi, 0, 0))
    return pl.pallas_call(
        functools.partial(_ret_kernel, chunk=chunk, unroll=unroll),
        grid=(b, h),
        in_specs=[pl.BlockSpec(memory_space=pltpu.SMEM),
                  hm(RET_DK, n), hm(RET_DK, n), hm(RET_DV, n),
                  pl.BlockSpec((1, n, RET_DV), lambda bi, hi: (bi, 0, hi)),
                  hm(RET_DK, nctx), hm(RET_DV, nctx)],
        out_specs=pl.BlockSpec((1, n, RET_DV), lambda bi, hi: (bi, 0, hi)),
        out_shape=jax.ShapeDtypeStruct((b, n, h * RET_DV), BF16),
        scratch_shapes=[pltpu.VMEM((n // chunk, 2 * RET_DK, RET_DV), F32),
                        pltpu.VMEM((n // chunk, RET_DK, 2 * RET_DV), BF16)],
        compiler_params=_cparams(("parallel", "parallel")),
        name="retention",
    )(dec, rq, rk, rv, rg, crk, crv)


def _mla_kernel(q_ref, k_ref, vp_ref, v_ref, ck_ref, cv_ref, o_ref,
                m_ref, acc_ref, s_ref, *, sub):
    j = pl.program_id(3)
    nj = pl.num_programs(3)
    nt = (((1,), (1,)), ((), ()))
    n_sub = q_ref.shape[2] // sub
    row_tiles = [slice(r * sub, (r + 1) * sub) for r in range(n_sub)]
    last = n_sub - 1

    def with_ones(v):
        return jnp.concatenate([v, jnp.ones_like(v)], axis=-1)

    def softmax_update(rows, s):
        m_old = m_ref[rows, :]
        m_new = jnp.maximum(m_old, jnp.max(s, axis=-1, keepdims=True))
        m_ref[rows, :] = m_new
        p = jnp.exp2(s - jnp.tile(m_new, (1, s.shape[1] // LANES))).astype(BF16)
        return jnp.exp2(m_old - m_new), p

    def accumulate(rows, alpha, p, v1):
        acc_ref[rows, :] = jnp.tile(alpha, (1, 2)) * acc_ref[rows, :] + jnp.dot(
            p, v1, preferred_element_type=F32)

    @pl.when(j == 0)
    def _():
        s = lax.dot_general(q_ref[0, 0], ck_ref[0, 0], nt, preferred_element_type=F32)
        m = jnp.max(s, axis=-1, keepdims=True)
        p = jnp.exp2(s - m)
        m_ref[...] = jnp.broadcast_to(m, m_ref.shape)
        acc_ref[...] = jnp.dot(p.astype(BF16), with_ones(cv_ref[0, 0]),
                               preferred_element_type=F32)
        s_ref[...] = jnp.full_like(s_ref, -jnp.inf)

    k = k_ref[0, 0]
    v_cur = with_ones(v_ref[0, 0])
    scores, alphas, probs = {}, {}, {}

    def stage_scores(r):
        scores[r] = lax.dot_general(q_ref[0, 0, row_tiles[r], :], k, nt,
                                    preferred_element_type=F32)

    def stage_softmax(r):
        alphas[r], probs[r] = softmax_update(row_tiles[r], scores.pop(r))

    def stage_values(r):
        accumulate(row_tiles[r], alphas.pop(r), probs.pop(r), v_cur)

    stage_scores(0)
    carried = softmax_update(row_tiles[last], s_ref[...])
    for t in range(1, n_sub + 1):
        if t < n_sub:
            stage_scores(t)
        if t - 1 < last:
            stage_softmax(t - 1)
        if t == 1:
            accumulate(row_tiles[last], *carried, with_ones(vp_ref[0, 0]))
        if 0 <= t - 2:
            stage_values(t - 2)
    s_ref[...] = scores.pop(last)

    @pl.when(j == nj - 1)
    def _():
        accumulate(row_tiles[last], *softmax_update(row_tiles[last], s_ref[...]), v_cur)
        acc = acc_ref[...]
        o_ref[0] = (acc[:, :MLA_V] / acc[:, MLA_V:]).astype(BF16)


def _mla(q, k, v, ck, cv, tq=8192, tk=2048, sub=512):
    b, h, n, w = q.shape
    nctx = ck.shape[2]
    tq = min(tq, n)
    tk = min(tk, n)
    sub = min(sub, tq)
    return pl.pallas_call(
        functools.partial(_mla_kernel, sub=sub),
        grid=(b, h, n // tq, n // tk),
        in_specs=[pl.BlockSpec((1, 1, tq, w), lambda bi, hi, i, j: (bi, hi, i, 0)),
                  pl.BlockSpec((1, 1, tk, w), lambda bi, hi, i, j: (bi, hi, j, 0)),
                  pl.BlockSpec((1, 1, tk, MLA_V),
                               lambda bi, hi, i, j: (bi, hi, jnp.maximum(j - 1, 0), 0)),
                  pl.BlockSpec((1, 1, tk, MLA_V), lambda bi, hi, i, j: (bi, hi, j, 0)),
                  pl.BlockSpec((1, 1, nctx, w), lambda bi, hi, i, j: (bi, hi, 0, 0)),
                  pl.BlockSpec((1, 1, nctx, MLA_V), lambda bi, hi, i, j: (bi, hi, 0, 0))],
        out_specs=pl.BlockSpec((1, tq, MLA_V), lambda bi, hi, i, j: (bi, i, hi)),
        out_shape=jax.ShapeDtypeStruct((b, n, h * MLA_V), BF16),
        scratch_shapes=[pltpu.VMEM((tq, LANES), F32), pltpu.VMEM((tq, 2 * MLA_V), F32),
                        pltpu.VMEM((sub, tk), F32)],
        compiler_params=_cparams(("parallel", "parallel", "parallel", "arbitrary")),
        name="mla_attn",
    )(q, k, v, v, ck, cv)


def _mixout_kernel(x_ref, mod_ref, ret_ref, mla_ref, w_ref, o_ref):
    hr = ret_ref.shape[1]
    mix = jnp.dot(ret_ref[...], w_ref[0:hr, :], preferred_element_type=F32)
    mix += jnp.dot(mla_ref[...], w_ref[hr:, :], preferred_element_type=F32)
    o_ref[...] = x_ref[...] + mod_ref[0, 2:3, :] * mix


def _mixout(x, mod, ret, mla, w, rows_per_mod, tm=512):
    r, d = x.shape
    tm = min(tm, r)
    bpm = rows_per_mod // tm
    return pl.pallas_call(
        _mixout_kernel,
        grid=(r // tm,),
        in_specs=[pl.BlockSpec((tm, d), lambda i: (i, 0)),
                  pl.BlockSpec((1, 3, d), lambda i: (i // bpm, 0, 0)),
                  pl.BlockSpec((tm, ret.shape[1]), lambda i: (i, 0)),
                  pl.BlockSpec((tm, mla.shape[1]), lambda i: (i, 0)),
                  _const_spec(w.shape)],
        out_specs=pl.BlockSpec((tm, d), lambda i: (i, 0)),
        out_shape=jax.ShapeDtypeStruct((r, d), F32),
        compiler_params=_cparams(("parallel",)),
        name="mixout",
    )(x, mod, ret, mla, w)


def _rope_tables(n):
    pos = np.arange(n, dtype=np.float64)
    inv_r = RET_ROPE_BASE ** (-np.arange(0, RET_DK, 2, dtype=np.float64) / RET_DK)
    ang = pos[:, None] * inv_r[None, :]
    c, s = np.cos(ang), np.sin(ang)
    rcos = np.tile(np.concatenate([c, c], -1), (1, LANES // RET_DK))
    rsin = np.tile(np.concatenate([-s, s], -1), (1, LANES // RET_DK))
    ax = MLA_ROPE // 2
    inv_a = AXIAL_BASE ** (-np.arange(0, ax, 2, dtype=np.float64) / ax)
    pr = (np.arange(n) // GRID_W).astype(np.float64)[:, None] * inv_a[None, :]
    pc = (np.arange(n) % GRID_W).astype(np.float64)[:, None] * inv_a[None, :]
    cr, sr, cc, sc = np.cos(pr), np.sin(pr), np.cos(pc), np.sin(pc)
    pad1 = np.ones((n, LANES - MLA_ROPE))
    pad0 = np.zeros((n, LANES - MLA_ROPE))
    acos = np.concatenate([cr, cr, cc, cc, pad1], -1)
    asin = np.concatenate([-sr, sr, -sc, sc, pad0], -1)
    return tuple(jnp.asarray(t.astype(np.float32)) for t in (rcos, rsin, acos, asin))


def kernel(x, c, ctx, c_ctx, ada_w, ada_b, norm1_g, ffn1_w_in, ffn1_w_out, norm2_g, mix_w_in,
           ret_decay_fwd, ret_decay_bwd, mla_q_norm_g, mla_w_uq, mla_kv_norm_g, mla_w_ukv,
           mix_w_out, norm3_g, ffn2_w_in, ffn2_w_out, final_norm_g):
    b, n, d = x.shape
    nctx = ctx.shape[1]
    depth = ada_w.shape[0]
    heads = ret_decay_fwd.shape[1]
    assert depth == 1, "single-layer block"
    assert mla_w_uq.shape[2] == heads * (MLA_NOPE + MLA_ROPE)
    l = 0

    cvec = jnp.concatenate([c, c_ctx[None, :]], 0)
    mods = _adaln(cvec, ada_w[l], ada_b[l]).reshape(b + 1, 9, d)
    m_lat, m_ctx = mods[:b], mods[b:b + 1]

    w1i, w1o = _ffn_weights(ffn1_w_in[l], ffn1_w_out[l])
    w2i, w2o = _ffn_weights(ffn2_w_in[l], ffn2_w_out[l])
    wmo = mix_w_out[l].astype(BF16)
    wmi = mix_w_in[l].astype(BF16)
    wuq = mla_w_uq[l].reshape(MLA_Q_RANK, heads, MLA_NOPE + MLA_ROPE)
    wuq = jnp.pad(wuq, ((0, 0), (0, 0), (0, MLA_QK_PAD - MLA_NOPE - MLA_ROPE)))
    wuq = wuq.reshape(MLA_Q_RANK, heads * MLA_QK_PAD).astype(BF16)
    wukv = mla_w_ukv[l].astype(BF16)

    x2 = x.reshape(b * n, d)
    c2 = ctx.reshape(b * nctx, d)

    x2 = _ffn(x2, m_lat[:, 0:3], norm1_g[l], w1i, w1o, rows_per_mod=n)
    c2 = _ffn(c2, m_ctx[:, 0:3], norm1_g[l], w1i, w1o, rows_per_mod=b * nctx)

    tables = _rope_tables(n)
    rq, rk, rv, rg, q, k, v = _proj(x2, m_lat[:, 3:6], norm2_g[l], wmi, mla_q_norm_g[l], wuq,
                                    mla_kv_norm_g[l], wukv, heads, b, tables)
    crk, crv, ck, cv = _proj(c2, m_ctx[:, 3:6], norm2_g[l], wmi, mla_q_norm_g[l], wuq,
                             mla_kv_norm_g[l], wukv, heads, b)
    dec = jnp.stack([ret_decay_fwd[l], ret_decay_bwd[l]]).astype(F32)
    ret = _retention(dec, rq, rk, rv, rg, crk, crv)
    mla = _mla(q, k, v, ck, cv)
    x2 = _mixout(x2, m_lat[:, 3:6], ret.reshape(b * n, -1), mla.reshape(b * n, -1), wmo,
                 rows_per_mod=n)

    x2 = _ffn(x2, m_lat[:, 6:9], norm3_g[l], w2i, w2o, rows_per_mod=n, final_g=final_norm_g)
    return x2.reshape(b, n, d)
```

```python
import functools
import math

import numpy as np

import jax
import jax.numpy as jnp
from jax import lax
from jax.experimental import pallas as pl
from jax.experimental.pallas import tpu as pltpu

GRID_W = 64
RET_DK = 64
RET_DV = 128
RET_ROPE_BASE = 10000.0
MLA_Q_RANK = 512
MLA_KV_RANK = 256
MLA_NOPE = 128
MLA_ROPE = 64
MLA_V = 128
AXIAL_BASE = 10000.0
RMS_EPS = 1e-6
GN_EPS = 1e-5

LANES = 128
ROW_SLICE = 16
MLA_QK_PAD = 2 * LANES
VMEM_LIMIT = 56 * 1024 * 1024

F32 = jnp.float32
BF16 = jnp.bfloat16


def _cparams(sem):
    return pltpu.CompilerParams(dimension_semantics=sem, vmem_limit_bytes=VMEM_LIMIT)


def _const_spec(shape):
    nd = len(shape)
    return pl.BlockSpec(shape, lambda *_: (0,) * nd, pipeline_mode=pl.Buffered(1))


def _silu(x):
    return x / (1.0 + jnp.exp(-x))


def _rms(x, g):
    return x * lax.rsqrt(jnp.mean(x * x, axis=-1, keepdims=True) + RMS_EPS) * g


def _ada_kernel(ct_ref, w_ref, b_ref, o_ref, sb_ref):
    m = sb_ref.shape[0]

    @pl.when(pl.program_id(0) == 0)
    def _():
        s = _silu(ct_ref[...])
        for r in range(m):
            sb_ref[r] = jnp.broadcast_to(s[:, r:r + 1], sb_ref.shape[1:])

    sub = 8
    col_tiles = [slice(t * LANES, (t + 1) * LANES) for t in range(w_ref.shape[1] // LANES)]

    def body(kc, accs):
        rows = pl.ds(pl.multiple_of(kc * sub, sub), sub)
        s_rows = [sb_ref[r, rows, :] for r in range(m)]
        out = []
        for t, cols in enumerate(col_tiles):
            wv = w_ref[rows, cols]
            out += [accs[t * m + r] + s_rows[r] * wv for r in range(m)]
        return tuple(out)

    zero = jnp.zeros((sub, LANES), F32)
    accs = lax.fori_loop(0, w_ref.shape[0] // sub, body, (zero,) * (m * len(col_tiles)),
                         unroll=4)
    for t, cols in enumerate(col_tiles):
        for r in range(m):
            o_ref[r:r + 1, cols] = jnp.sum(accs[t * m + r], axis=0, keepdims=True) + b_ref[:, cols]


def _adaln(cvec, ada_w, ada_b, tn_max=1024):
    m, d = cvec.shape
    n = ada_w.shape[1]
    tn = max(t for t in range(LANES, tn_max + 1, LANES) if n % t == 0)
    return pl.pallas_call(
        _ada_kernel,
        grid=(n // tn,),
        in_specs=[pl.BlockSpec((d, m), lambda j: (0, 0)),
                  pl.BlockSpec((d, tn), lambda j: (0, j)),
                  pl.BlockSpec((1, tn), lambda j: (0, j))],
        out_specs=pl.BlockSpec((m, tn), lambda j: (0, j)),
        out_shape=jax.ShapeDtypeStruct((m, n), F32),
        scratch_shapes=[pltpu.VMEM((m, d, LANES), F32)],
        compiler_params=_cparams(("arbitrary",)),
        name="adaln",
    )(cvec.T, ada_w, ada_b.reshape(1, n))


def _ffn_kernel(x_ref, mod_ref, g_ref, wi_hbm, wo_hbm, *rest, final):
    if final:
        fg_ref, o_ref, xn_ref, acc_ref, wi_buf, wo_buf, sem = rest
    else:
        o_ref, xn_ref, acc_ref, wi_buf, wo_buf, sem = rest
    i = pl.program_id(0)
    n_blocks = pl.num_programs(0)
    nf, tf = wo_hbm.shape[0], wo_hbm.shape[1]
    n_slices = x_ref.shape[0] // ROW_SLICE

    def copies(c, slot):
        return (pltpu.make_async_copy(wi_hbm.at[c], wi_buf.at[slot], sem.at[0, slot]),
                pltpu.make_async_copy(wo_hbm.at[c], wo_buf.at[slot], sem.at[1, slot]))

    def row_slice(r):
        return pl.ds(pl.multiple_of(r * ROW_SLICE, ROW_SLICE), ROW_SLICE)

    @pl.when(i == 0)
    def _():
        for cp in copies(0, 0):
            cp.start()

    gain = g_ref[...] * (1.0 + mod_ref[0, 1:2, :])
    shift = mod_ref[0, 0:1, :]

    def prenorm(r, carry):
        xr = x_ref[row_slice(r), :]
        rs = lax.rsqrt(jnp.mean(xr * xr, axis=-1, keepdims=True) + RMS_EPS)
        xn_ref[row_slice(r), :] = (xr * rs * gain + shift).astype(BF16)
        acc_ref[row_slice(r), :] = jnp.zeros((ROW_SLICE, acc_ref.shape[1]), F32)
        return carry

    lax.fori_loop(0, n_slices, prenorm, 0, unroll=8)

    def chunk(c, carry):
        step = i * nf + c
        slot = step % 2

        @pl.when(step + 1 < n_blocks * nf)
        def _():
            for cp in copies(jnp.where(c + 1 == nf, 0, c + 1), 1 - slot):
                cp.start()

        for cp in copies(c, slot):
            cp.wait()
        h = jnp.dot(xn_ref[...], wi_buf[slot], preferred_element_type=F32)
        act = (_silu(h[:, :tf]) * h[:, tf:]).astype(BF16)
        acc_ref[...] += jnp.dot(act, wo_buf[slot], preferred_element_type=F32)
        return carry

    lax.fori_loop(0, nf, chunk, 0)

    half_gate = 0.5 * mod_ref[0, 2:3, :]
    if final:
        def residual_norm(r, carry):
            out = x_ref[row_slice(r), :] + half_gate * acc_ref[row_slice(r), :]
            o_ref[row_slice(r), :] = _rms(out, fg_ref[...])
            return carry

        lax.fori_loop(0, n_slices, residual_norm, 0, unroll=8)
    else:
        o_ref[...] = x_ref[...] + half_gate * acc_ref[...]


FFN_TF = 512


def _retile_kernel(g_ref, u_ref, o_ref):
    tf = g_ref.shape[1]
    o_ref[0, :, :tf] = g_ref[...].astype(BF16)
    o_ref[0, :, tf:] = u_ref[...].astype(BF16)


def _ffn_weights(w_in, w_out, tf=FFN_TF):
    d, f2 = w_in.shape
    nf = f2 // 2 // tf
    w = pl.pallas_call(
        _retile_kernel,
        grid=(nf,),
        in_specs=[pl.BlockSpec((d, tf), lambda j: (0, j)),
                  pl.BlockSpec((d, tf), lambda j: (0, j + nf))],
        out_specs=pl.BlockSpec((1, d, 2 * tf), lambda j: (j, 0, 0)),
        out_shape=jax.ShapeDtypeStruct((nf, d, 2 * tf), BF16),
        compiler_params=_cparams(("parallel",)),
        name="ffn_w_retile",
    )(w_in, w_in)
    return w, w_out.astype(BF16)


def _ffn(x, mod, g, w_in, w_out, rows_per_mod, final_g=None, tm=512, tf=FFN_TF):
    r, d = x.shape
    f = w_out.shape[0]
    tm = min(tm, r)
    nf = f // tf
    bpm = rows_per_mod // tm
    final = final_g is not None
    in_specs = [
        pl.BlockSpec((tm, d), lambda i: (i, 0)),
        pl.BlockSpec((1, 3, d), lambda i: (i // bpm, 0, 0)),
        pl.BlockSpec((1, d), lambda i: (0, 0)),
        pl.BlockSpec(memory_space=pl.ANY),
        pl.BlockSpec(memory_space=pl.ANY),
    ]
    args = [x, mod, g.reshape(1, d), w_in, w_out.reshape(nf, tf, d)]
    if final:
        in_specs.append(pl.BlockSpec((1, d), lambda i: (0, 0)))
        args.append(final_g.reshape(1, d))
    return pl.pallas_call(
        functools.partial(_ffn_kernel, final=final),
        grid=(r // tm,),
        in_specs=in_specs,
        out_specs=pl.BlockSpec((tm, d), lambda i: (i, 0)),
        out_shape=jax.ShapeDtypeStruct((r, d), F32),
        scratch_shapes=[pltpu.VMEM((tm, d), BF16), pltpu.VMEM((tm, d), F32),
                        pltpu.VMEM((2, d, 2 * tf), BF16), pltpu.VMEM((2, tf, d), BF16),
                        pltpu.SemaphoreType.DMA((2, 2))],
        compiler_params=_cparams(("arbitrary",)),
        name="ffn_final" if final else "ffn",
    )(*args)


def _swap_halves(x, half):
    lane = lax.broadcasted_iota(jnp.int32, x.shape, 1)
    first = (lane % (2 * half)) < half
    return jnp.where(first, pltpu.roll(x, LANES - half, 1), pltpu.roll(x, half, 1))


def _rope(x, cos, sin, half):
    return x * cos + _swap_halves(x, half) * sin


def _proj_kernel(x_ref, mod_ref, g_ref, w_ref, qg_ref, wuq_ref, kvg_ref, wukv_ref, *rest,
                 heads, latent):
    if latent:
        (rcos_ref, rsin_ref, acos_ref, asin_ref,
         rq_ref, rk_ref, rv_ref, rg_ref, q_ref, k_ref, v_ref) = rest
    else:
        rk_ref, rv_ref, k_ref, v_ref = rest
    hk = heads * RET_DK
    hv = heads * RET_DV
    o_rk, o_rv, o_rg = hk, 2 * hk, 2 * hk + hv
    o_cq = o_rg + hv
    o_ckv = o_cq + MLA_Q_RANK
    o_kr = o_ckv + MLA_KV_RANK

    y = _rms(x_ref[...], g_ref[...])
    h = (y * (1.0 + mod_ref[0, 1:2, :]) + mod_ref[0, 0:1, :]).astype(BF16)
    p = jnp.dot(h, w_ref[...], preferred_element_type=F32)

    for gidx in range(hk // LANES):
        sl = slice(gidx * LANES, (gidx + 1) * LANES)
        kk = p[:, o_rk + gidx * LANES:o_rk + (gidx + 1) * LANES] * (RET_DK ** -0.5)
        if latent:
            kk = _rope(kk, rcos_ref[...], rsin_ref[...], RET_DK // 2)
            qq = _rope(p[:, sl], rcos_ref[...], rsin_ref[...], RET_DK // 2)
        for t in range(LANES // RET_DK):
            hd = gidx * (LANES // RET_DK) + t
            rk_ref[0, hd] = kk[:, t * RET_DK:(t + 1) * RET_DK].astype(BF16)
            if latent:
                rq_ref[0, hd] = qq[:, t * RET_DK:(t + 1) * RET_DK].astype(BF16)
    for hd in range(heads):
        rv_ref[0, hd] = p[:, o_rv + hd * RET_DV:o_rv + (hd + 1) * RET_DV].astype(BF16)
    if latent:
        rg_ref[0] = p[:, o_rg:o_rg + hv].astype(BF16)

    kvn = _rms(p[:, o_ckv:o_ckv + MLA_KV_RANK], kvg_ref[...]).astype(BF16)
    kv = jnp.dot(kvn, wukv_ref[...], preferred_element_type=F32)
    kr = p[:, o_kr:o_kr + MLA_ROPE]
    kr = jnp.concatenate([kr, jnp.zeros((kr.shape[0], LANES - MLA_ROPE), F32)], axis=-1)
    if latent:
        kr = _rope(kr, acos_ref[...], asin_ref[...], MLA_ROPE // 4)
    kr = kr.astype(BF16)
    for hd in range(heads):
        base = hd * (MLA_NOPE + MLA_V)
        k_ref[0, hd, :, 0:MLA_NOPE] = kv[:, base:base + MLA_NOPE].astype(BF16)
        k_ref[0, hd, :, MLA_NOPE:MLA_QK_PAD] = kr
        v_ref[0, hd] = kv[:, base + MLA_NOPE:base + MLA_NOPE + MLA_V].astype(BF16)
    if latent:
        qn = _rms(p[:, o_cq:o_cq + MLA_Q_RANK], qg_ref[...]).astype(BF16)
        q = jnp.dot(qn, wuq_ref[...], preferred_element_type=F32)
        scale = (MLA_NOPE + MLA_ROPE) ** -0.5 * math.log2(math.e)
        for hd in range(heads):
            base = hd * MLA_QK_PAD
            q_ref[0, hd, :, 0:MLA_NOPE] = (q[:, base:base + MLA_NOPE] * scale).astype(BF16)
            qr = _rope(q[:, base + MLA_NOPE:base + MLA_QK_PAD],
                       acos_ref[...], asin_ref[...], MLA_ROPE // 4)
            q_ref[0, hd, :, MLA_NOPE:MLA_QK_PAD] = (qr * scale).astype(BF16)


def _proj(x, mod, g, w_in_p, qg, wuq_p, kvg, wukv, heads, batch, tables=None, tm=256):
    r, d = x.shape
    n = r // batch
    tm = min(tm, n)
    nb = n // tm
    latent = tables is not None
    row = lambda i: (i, 0)
    cst2 = lambda i: (0, 0)
    in_specs = [
        pl.BlockSpec((tm, d), row),
        pl.BlockSpec((1, 3, d), lambda i: (i // nb if latent else 0, 0, 0)),
        pl.BlockSpec((1, d), cst2),
        _const_spec(w_in_p.shape),
        pl.BlockSpec((1, MLA_Q_RANK), cst2),
        _const_spec(wuq_p.shape),
        pl.BlockSpec((1, MLA_KV_RANK), cst2),
        _const_spec(wukv.shape),
    ]
    args = [x, mod, g.reshape(1, d), w_in_p, qg.reshape(1, -1), wuq_p, kvg.reshape(1, -1), wukv]
    hmaj = lambda w: pl.BlockSpec((1, heads, tm, w), lambda i: (i // nb, 0, i % nb, 0))
    hshape = lambda w: jax.ShapeDtypeStruct((batch, heads, n, w), BF16)
    if latent:
        in_specs += [pl.BlockSpec((tm, LANES), lambda i: (i % nb, 0))] * 4
        args += list(tables)
        out_specs = [hmaj(RET_DK), hmaj(RET_DK), hmaj(RET_DV),
                     pl.BlockSpec((1, tm, heads * RET_DV), lambda i: (i // nb, i % nb, 0)),
                     hmaj(MLA_QK_PAD), hmaj(MLA_QK_PAD), hmaj(MLA_V)]
        out_shape = [hshape(RET_DK), hshape(RET_DK), hshape(RET_DV),
                     jax.ShapeDtypeStruct((batch, n, heads * RET_DV), BF16),
                     hshape(MLA_QK_PAD), hshape(MLA_QK_PAD), hshape(MLA_V)]
    else:
        out_specs = [hmaj(RET_DK), hmaj(RET_DV), hmaj(MLA_QK_PAD), hmaj(MLA_V)]
        out_shape = [hshape(RET_DK), hshape(RET_DV), hshape(MLA_QK_PAD), hshape(MLA_V)]
    return pl.pallas_call(
        functools.partial(_proj_kernel, heads=heads, latent=latent),
        grid=(r // tm,),
        in_specs=in_specs,
        out_specs=out_specs,
        out_shape=out_shape,
        compiler_params=_cparams(("parallel",)),
        name="proj_lat" if latent else "proj_ctx",
    )(*args)


def _log_sigmoid(x):
    return jnp.minimum(x, 0.0) - jnp.log1p(jnp.exp(-jnp.abs(x)))


def _ret_kernel(dec_ref, q_ref, k_ref, v_ref, g_ref, ck_ref, cv_ref, o_ref, kv_ref, st_ref,
                *, chunk, unroll):
    hd = pl.program_id(1)
    n = q_ref.shape[2]
    nctx = ck_ref.shape[2]
    nc = n // chunk
    dk = RET_DK
    lgf = _log_sigmoid(jnp.full((1, 1), dec_ref[0, hd], F32))
    lgb = _log_sigmoid(jnp.full((1, 1), dec_ref[1, hd], F32))

    def col(m):
        return lax.broadcasted_iota(jnp.int32, (m, 1), 0).astype(F32)

    def weighted_kv(k, v, zf, zb):
        kf = k.astype(F32)
        kz = jnp.concatenate([kf * zf, kf * zb], axis=-1).astype(BF16)
        return lax.dot_general(kz, v, (((0,), (0,)), ((), ())), preferred_element_type=F32)

    ic = col(nctx)
    s0 = weighted_kv(ck_ref[0, 0], cv_ref[0, 0],
                     jnp.exp(lgf * (nctx - 1.0 - ic)), jnp.exp(lgb * ic))

    i = col(chunk)
    zeta_f = jnp.exp(lgf * (chunk - 1.0 - i))
    zeta_b = jnp.exp(lgb * i)
    xi_f = jnp.exp(lgf * (i + 1.0))
    xi_b = jnp.exp(lgb * (chunk - i))
    dec_f = jnp.exp(lgf * float(chunk))
    dec_b = jnp.exp(lgb * float(chunk))
    diff = i - lax.broadcasted_iota(jnp.int32, (1, chunk), 1).astype(F32)
    dmat = jnp.where(diff >= 0.0, jnp.exp(lgf * jnp.maximum(diff, 0.0)),
                     jnp.exp(lgb * jnp.maximum(-diff, 0.0)))

    def rows(c):
        return pl.ds(pl.multiple_of(c * chunk, chunk), chunk)

    def p1(c, carry):
        kv_ref[c] = weighted_kv(k_ref[0, 0, rows(c), :], v_ref[0, 0, rows(c), :], zeta_f, zeta_b)
        return carry
    lax.fori_loop(0, nc, p1, 0, unroll=unroll)

    dv = v_ref.shape[3]

    def p2f(c, s):
        st_ref[c, :, 0:dv] = s.astype(BF16)
        return dec_f * s + kv_ref[c, 0:dk, :]
    lax.fori_loop(0, nc, p2f, s0[0:dk])

    def p2b(t, s):
        c = nc - 1 - t
        st_ref[c, :, dv:2 * dv] = s.astype(BF16)
        return dec_b * s + kv_ref[c, dk:2 * dk, :]
    lax.fori_loop(0, nc, p2b, s0[dk:2 * dk])

    def p3(c, carry):
        q = q_ref[0, 0, rows(c), :]
        k = k_ref[0, 0, rows(c), :]
        v = v_ref[0, 0, rows(c), :]
        a = lax.dot_general(q, k, (((1,), (1,)), ((), ())), preferred_element_type=F32)
        y = jnp.dot((a * dmat).astype(BF16), v, preferred_element_type=F32)
        cross = jnp.dot(q, st_ref[c], preferred_element_type=F32)
        y += xi_f * cross[:, 0:dv] + xi_b * cross[:, dv:2 * dv]
        mu = jnp.mean(y, axis=-1, keepdims=True)
        yc = y - mu
        var = jnp.mean(yc * yc, axis=-1, keepdims=True)
        yn = yc * lax.rsqrt(var + GN_EPS)
        o_ref[0, rows(c), :] = (_silu(g_ref[0, rows(c), :].astype(F32)) * yn).astype(BF16)
        return carry
    lax.fori_loop(0, nc, p3, 0, unroll=unroll)


def _retention(dec, rq, rk, rv, rg, crk, crv, chunk=256, unroll=8):
    b, h, n, _ = rq.shape
    nctx = crk.shape[2]
    chunk = min(chunk, n)
    unroll = math.gcd(unroll, n // chunk)
    hm = lambda w, rows: pl.BlockSpec((1, 1, rows, w), lambda bi, hi: (bi, hi, 0, 0))
    return pl.pallas_call(
        functools.partial(_ret_kernel, chunk=chunk, unroll=unroll),
        grid=(b, h),
        in_specs=[pl.BlockSpec(memory_space=pltpu.SMEM),
                  hm(RET_DK, n), hm(RET_DK, n), hm(RET_DV, n),
                  pl.BlockSpec((1, n, RET_DV), lambda bi, hi: (bi, 0, hi)),
                  hm(RET_DK, nctx), hm(RET_DV, nctx)],
        out_specs=pl.BlockSpec((1, n, RET_DV), lambda bi, hi: (bi, 0, hi)),
        out_shape=jax.ShapeDtypeStruct((b, n, h * RET_DV), BF16),
        scratch_shapes=[pltpu.VMEM((n // chunk, 2 * RET_DK, RET_DV), F32),
                        pltpu.VMEM((n // chunk, RET_DK, 2 * RET_DV), BF16)],
        compiler_params=_cparams(("parallel", "parallel")),
        name="retention",
    )(dec, rq, rk, rv, rg, crk, crv)


def _mla_kernel(q_ref, k_ref, vp_ref, v_ref, ck_ref, cv_ref, o_ref,
                m_ref, acc_ref, s_ref, *, sub):
    j = pl.program_id(3)
    nj = pl.num_programs(3)
    nt = (((1,), (1,)), ((), ()))
    n_sub = q_ref.shape[2] // sub
    row_tiles = [slice(r * sub, (r + 1) * sub) for r in range(n_sub)]
    last = n_sub - 1

    def with_ones(v):
        return jnp.concatenate([v, jnp.ones_like(v)], axis=-1)

    def softmax_update(rows, s):
        m_old = m_ref[rows, :]
        m_new = jnp.maximum(m_old, jnp.max(s, axis=-1, keepdims=True))
        m_ref[rows, :] = m_new
        p = jnp.exp2(s - jnp.tile(m_new, (1, s.shape[1] // LANES))).astype(BF16)
        return jnp.exp2(m_old - m_new), p

    def accumulate(rows, alpha, p, v1):
        acc_ref[rows, :] = jnp.tile(alpha, (1, 2)) * acc_ref[rows, :] + jnp.dot(
            p, v1, preferred_element_type=F32)

    @pl.when(j == 0)
    def _():
        s = lax.dot_general(q_ref[0, 0], ck_ref[0, 0], nt, preferred_element_type=F32)
        m = jnp.max(s, axis=-1, keepdims=True)
        p = jnp.exp2(s - m)
        m_ref[...] = jnp.broadcast_to(m, m_ref.shape)
        acc_ref[...] = jnp.dot(p.astype(BF16), with_ones(cv_ref[0, 0]),
                               preferred_element_type=F32)
        s_ref[...] = jnp.full_like(s_ref, -jnp.inf)

    k = k_ref[0, 0]
    v_cur = with_ones(v_ref[0, 0])
    scores, alphas, probs = {}, {}, {}

    def stage_scores(r):
        scores[r] = lax.dot_general(q_ref[0, 0, row_tiles[r], :], k, nt,
                                    preferred_element_type=F32)

    def stage_softmax(r):
        alphas[r], probs[r] = softmax_update(row_tiles[r], scores.pop(r))

    def stage_values(r):
        accumulate(row_tiles[r], alphas.pop(r), probs.pop(r), v_cur)

    stage_scores(0)
    carried = softmax_update(row_tiles[last], s_ref[...])
    for t in range(1, n_sub + 1):
        if t < n_sub:
            stage_scores(t)
        if t - 1 < last:
            stage_softmax(t - 1)
        if t == 1:
            accumulate(row_tiles[last], *carried, with_ones(vp_ref[0, 0]))
        if 0 <= t - 2:
            stage_values(t - 2)
    s_ref[...] = scores.pop(last)

    @pl.when(j == nj - 1)
    def _():
        accumulate(row_tiles[last], *softmax_update(row_tiles[last], s_ref[...]), v_cur)
        acc = acc_ref[...]
        o_ref[0] = (acc[:, :MLA_V] / acc[:, MLA_V:]).astype(BF16)


def _mla(q, k, v, ck, cv, tq=8192, tk=2048, sub=512):
    b, h, n, w = q.shape
    nctx = ck.shape[2]
    tq = min(tq, n)
    tk = min(tk, n)
    sub = min(sub, tq)
    return pl.pallas_call(
        functools.partial(_mla_kernel, sub=sub),
        grid=(b, h, n // tq, n // tk),
        in_specs=[pl.BlockSpec((1, 1, tq, w), lambda bi, hi, i, j: (bi, hi, i, 0)),
                  pl.BlockSpec((1, 1, tk, w), lambda bi, hi, i, j: (bi, hi, j, 0)),
                  pl.BlockSpec((1, 1, tk, MLA_V),
                               lambda bi, hi, i, j: (bi, hi, jnp.maximum(j - 1, 0), 0)),
                  pl.BlockSpec((1, 1, tk, MLA_V), lambda bi, hi, i, j: (bi, hi, j, 0)),
                  pl.BlockSpec((1, 1, nctx, w), lambda bi, hi, i, j: (bi, hi, 0, 0)),
                  pl.BlockSpec((1, 1, nctx, MLA_V), lambda bi, hi, i, j: (bi, hi, 0, 0))],
        out_specs=pl.BlockSpec((1, tq, MLA_V), lambda bi, hi, i, j: (bi, i, hi)),
        out_shape=jax.ShapeDtypeStruct((b, n, h * MLA_V), BF16),
        scratch_shapes=[pltpu.VMEM((tq, LANES), F32), pltpu.VMEM((tq, 2 * MLA_V), F32),
                        pltpu.VMEM((sub, tk), F32)],
        compiler_params=_cparams(("parallel", "parallel", "parallel", "arbitrary")),
        name="mla_attn",
    )(q, k, v, v, ck, cv)


def _mixout_kernel(x_ref, mod_ref, ret_ref, mla_ref, w_ref, o_ref):
    hr = ret_ref.shape[1]
    mix = jnp.dot(ret_ref[...], w_ref[0:hr, :], preferred_element_type=F32)
    mix += jnp.dot(mla_ref[...], w_ref[hr:, :], preferred_element_type=F32)
    o_ref[...] = x_ref[...] + mod_ref[0, 2:3, :] * mix


def _mixout(x, mod, ret, mla, w, rows_per_mod, tm=512):
    r, d = x.shape
    tm = min(tm, r)
    bpm = rows_per_mod // tm
    return pl.pallas_call(
        _mixout_kernel,
        grid=(r // tm,),
        in_specs=[pl.BlockSpec((tm, d), lambda i: (i, 0)),
                  pl.BlockSpec((1, 3, d), lambda i: (i // bpm, 0, 0)),
                  pl.BlockSpec((tm, ret.shape[1]), lambda i: (i, 0)),
                  pl.BlockSpec((tm, mla.shape[1]), lambda i: (i, 0)),
                  _const_spec(w.shape)],
        out_specs=pl.BlockSpec((tm, d), lambda i: (i, 0)),
        out_shape=jax.ShapeDtypeStruct((r, d), F32),
        compiler_params=_cparams(("parallel",)),
        name="mixout",
    )(x, mod, ret, mla, w)


def _rope_tables(n):
    pos = np.arange(n, dtype=np.float64)
    inv_r = RET_ROPE_BASE ** (-np.arange(0, RET_DK, 2, dtype=np.float64) / RET_DK)
    ang = pos[:, None] * inv_r[None, :]
    c, s = np.cos(ang), np.sin(ang)
    rcos = np.tile(np.concatenate([c, c], -1), (1, LANES // RET_DK))
    rsin = np.tile(np.concatenate([-s, s], -1), (1, LANES // RET_DK))
    ax = MLA_ROPE // 2
    inv_a = AXIAL_BASE ** (-np.arange(0, ax, 2, dtype=np.float64) / ax)
    pr = (np.arange(n) // GRID_W).astype(np.float64)[:, None] * inv_a[None, :]
    pc = (np.arange(n) % GRID_W).astype(np.float64)[:, None] * inv_a[None, :]
    cr, sr, cc, sc = np.cos(pr), np.sin(pr), np.cos(pc), np.sin(pc)
    pad1 = np.ones((n, LANES - MLA_ROPE))
    pad0 = np.zeros((n, LANES - MLA_ROPE))
    acos = np.concatenate([cr, cr, cc, cc, pad1], -1)
    asin = np.concatenate([-sr, sr, -sc, sc, pad0], -1)
    return tuple(jnp.asarray(t.astype(np.float32)) for t in (rcos, rsin, acos, asin))


def kernel(x, c, ctx, c_ctx, ada_w, ada_b, norm1_g, ffn1_w_in, ffn1_w_out, norm2_g, mix_w_in,
           ret_decay_fwd, ret_decay_bwd, mla_q_norm_g, mla_w_uq, mla_kv_norm_g, mla_w_ukv,
           mix_w_out, norm3_g, ffn2_w_in, ffn2_w_out, final_norm_g):
    b, n, d = x.shape
    nctx = ctx.shape[1]
    depth = ada_w.shape[0]
    heads = ret_decay_fwd.shape[1]
    assert depth == 1, "single-layer block"
    assert mla_w_uq.shape[2] == heads * (MLA_NOPE + MLA_ROPE)
    l = 0

    cvec = jnp.concatenate([c, c_ctx[None, :]], 0)
    mods = _adaln(cvec, ada_w[l], ada_b[l]).reshape(b + 1, 9, d)
    m_lat, m_ctx = mods[:b], mods[b:b + 1]

    w1i, w1o = _ffn_weights(ffn1_w_in[l], ffn1_w_out[l])
    w2i, w2o = _ffn_weights(ffn2_w_in[l], ffn2_w_out[l])
    wmo = mix_w_out[l].astype(BF16)
    wmi = mix_w_in[l].astype(BF16)
    wuq = mla_w_uq[l].reshape(MLA_Q_RANK, heads, MLA_NOPE + MLA_ROPE)
    wuq = jnp.pad(wuq, ((0, 0), (0, 0), (0, MLA_QK_PAD - MLA_NOPE - MLA_ROPE)))
    wuq = wuq.reshape(MLA_Q_RANK, heads * MLA_QK_PAD).astype(BF16)
    wukv = mla_w_ukv[l].astype(BF16)

    x2 = x.reshape(b * n, d)
    c2 = ctx.reshape(b * nctx, d)

    x2 = _ffn(x2, m_lat[:, 0:3], norm1_g[l], w1i, w1o, rows_per_mod=n)
    c2 = _ffn(c2, m_ctx[:, 0:3], norm1_g[l], w1i, w1o, rows_per_mod=b * nctx)

    tables = _rope_tables(n)
    rq, rk, rv, rg, q, k, v = _proj(x2, m_lat[:, 3:6], norm2_g[l], wmi, mla_q_norm_g[l], wuq,
                                    mla_kv_norm_g[l], wukv, heads, b, tables)
    crk, crv, ck, cv = _proj(c2, m_ctx[:, 3:6], norm2_g[l], wmi, mla_q_norm_g[l], wuq,
                             mla_kv_norm_g[l], wukv, heads, b)
    dec = jnp.stack([ret_decay_fwd[l], ret_decay_bwd[l]]).astype(F32)
    ret = _retention(dec, rq, rk, rv, rg, crk, crv)
    mla = _mla(q, k, v, ck, cv)
    x2 = _mixout(x2, m_lat[:, 3:6], ret.reshape(b * n, -1), mla.reshape(b * n, -1), wmo,
                 rows_per_mod=n)

    x2 = _ffn(x2, m_lat[:, 6:9], norm3_g[l], w2i, w2o, rows_per_mod=n, final_g=final_norm_g)
    return x2.reshape(b, n, d)
```
